```python
import math
import jax, jax.numpy as jnp
from jax import lax
import numpy as np

D_MODEL = 1024
BATCH = 16
SEQ = 4096
DEPTH = 2

CHUNK = 64
RMS_EPS = 1e-6
DN_WIDTH = D_MODEL // 2
DN_DHEAD = 128
DN_HEADS = DN_WIDTH // DN_DHEAD
DN_CONV = 4
SA_WIDTH = D_MODEL // 4
SA_DHEAD = 64
SA_HEADS = SA_WIDTH // SA_DHEAD
SA_DKV = SA_DHEAD
IDX_HEADS = 4
IDX_DHEAD = 64
IDX_TOPK_MAX = 256
Q_BLOCK = 128
HG_WIDTH = D_MODEL // 4
HG_DHEAD = 64
HG_HEADS = HG_WIDTH // HG_DHEAD
D_MIX = DN_WIDTH + SA_WIDTH + HG_WIDTH
D_FF = 2816
FFN_CONV = 3

SPLIT_SIZES = (DN_WIDTH, DN_WIDTH, DN_WIDTH, DN_WIDTH, DN_HEADS, DN_HEADS,
               SA_WIDTH, SA_DKV, SA_DKV, IDX_HEADS * IDX_DHEAD, IDX_DHEAD, IDX_HEADS,
               HG_WIDTH, HG_WIDTH, HG_WIDTH, HG_WIDTH)
D_IN = sum(SPLIT_SIZES)
SPLIT_POINTS = tuple(int(p) for p in np.cumsum(SPLIT_SIZES)[:-1])

kernel_name = 'hybrid_parallel_heads_streaming_encoder'


def rmsnorm(x, g):
    xf = x.astype(jnp.float32)
    y = xf * lax.rsqrt(jnp.mean(xf * xf, axis=-1, keepdims=True) + RMS_EPS)
    return (y * g.astype(jnp.float32)).astype(x.dtype)


def l2norm(x):
    return x * lax.rsqrt(jnp.sum(x * x, axis=-1, keepdims=True) + 1e-6)


def causal_depthwise_conv(x, w):
    K, C = w.shape
    return lax.conv_general_dilated(x, w[:, None, :].astype(x.dtype), window_strides=(1,),
                                    padding=[(K - 1, 0)],
                                    dimension_numbers=('NWC', 'WIO', 'NWC'),
                                    feature_group_count=C)


def to_chunks(t, n_heads, d_head):
    B, S = t.shape[:2]
    return t.reshape(B, S // CHUNK, CHUNK, n_heads, d_head).transpose(0, 3, 1, 2, 4)


def gate_chunks(t):
    B, S, H = t.shape
    return t.reshape(B, S // CHUNK, CHUNK, H).transpose(0, 3, 1, 2)


def from_chunks(o):
    N, B, H, C, d = o.shape
    return o.transpose(1, 0, 3, 2, 4).reshape(B, N * C, H, d)


def gated_deltanet(q, k, v, z, b, a, conv_w, a_log, dt_bias, norm_g):
    dtype = q.dtype
    Bsz, S, _ = q.shape
    qkv = jax.nn.silu(causal_depthwise_conv(jnp.concatenate([q, k, v], axis=-1), conv_w))
    q, k, v = jnp.split(qkv.astype(jnp.float32), 3, axis=-1)
    q = l2norm(to_chunks(q, DN_HEADS, DN_DHEAD)) * (DN_DHEAD ** -0.5)
    k = l2norm(to_chunks(k, DN_HEADS, DN_DHEAD))
    v = to_chunks(v, DN_HEADS, DN_DHEAD)
    beta = gate_chunks(jax.nn.sigmoid(b.astype(jnp.float32)))
    g = -jnp.exp(a_log.astype(jnp.float32)) * jax.nn.softplus(
        a.astype(jnp.float32) + dt_bias.astype(jnp.float32))
    G = jnp.cumsum(gate_chunks(g), axis=-1)
    causal = jnp.tril(jnp.ones((CHUNK, CHUNK), bool))
    strict = jnp.tril(jnp.ones((CHUNK, CHUNK), bool), k=-1)
    decay = jnp.exp(jnp.where(causal, G[..., :, None] - G[..., None, :], -jnp.inf))
    kk = jnp.einsum('bhnck,bhnsk->bhncs', k, k)
    M = jnp.where(strict, beta[..., :, None] * kk * decay, 0.0)
    eye = jnp.eye(CHUNK, dtype=jnp.float32)
    T = lax.linalg.triangular_solve(eye + M, jnp.broadcast_to(eye, M.shape),
                                    left_side=True, lower=True, unit_diagonal=True)
    u0 = jnp.einsum('bhncs,bhnsv->bhncv', T, beta[..., None] * v)
    wk = jnp.einsum('bhncs,bhnsk->bhnck', T, (beta * jnp.exp(G))[..., None] * k)
    qk = jnp.einsum('bhnck,bhnsk->bhncs', q, k) * decay
    q_dec = q * jnp.exp(G)[..., None]
    k_dec = k * jnp.exp(G[..., -1:] - G)[..., None]
    g_last = jnp.exp(G[..., -1])

    def step(state, inp):
        u0_c, w_c, qk_c, qd_c, kd_c, gl_c = inp
        u = u0_c - jnp.einsum('bhck,bhkv->bhcv', w_c, state)
        o = jnp.einsum('bhck,bhkv->bhcv', qd_c, state) + jnp.einsum('bhcs,bhsv->bhcv', qk_c, u)
        state = state * gl_c[..., None, None] + jnp.einsum('bhck,bhcv->bhkv', kd_c, u)
        return state, o

    xs = tuple(jnp.moveaxis(t, 2, 0) for t in (u0, wk, qk, q_dec, k_dec, g_last))
    s0 = jnp.zeros((Bsz, DN_HEADS, DN_DHEAD, DN_DHEAD), jnp.float32)
    _, o = lax.scan(step, s0, xs)
    o = rmsnorm(from_chunks(o), norm_g)
    o = o * jax.nn.silu(z.astype(jnp.float32).reshape(Bsz, S, DN_HEADS, DN_DHEAD))
    return o.reshape(Bsz, S, DN_WIDTH).astype(dtype)


def dsa_attention(q, k, v, qi, ki, wi):
    dtype = q.dtype
    Bsz, S, _ = q.shape
    nb = S // Q_BLOCK
    topk = min(IDX_TOPK_MAX, S // 4)
    f32 = jnp.float32
    qb_all = q.astype(f32).reshape(Bsz, nb, Q_BLOCK, SA_HEADS, SA_DHEAD).transpose(1, 0, 2, 3, 4)
    qib_all = qi.astype(f32).reshape(Bsz, nb, Q_BLOCK, IDX_HEADS, IDX_DHEAD).transpose(1, 0, 2, 3, 4)
    wib_all = (wi.astype(f32) * (IDX_HEADS ** -0.5 * IDX_DHEAD ** -0.5)).reshape(
        Bsz, nb, Q_BLOCK, IDX_HEADS).transpose(1, 0, 2, 3)
    k = k.astype(f32)
    v = v.astype(f32)
    ki = ki.astype(f32)
    key_chunk = jnp.arange(S) // CHUNK
    gather = jax.vmap(lambda t, idx: t[idx])

    def block(args):
        qb, qib, wib, blk = args
        q_chunk = (blk * Q_BLOCK + jnp.arange(Q_BLOCK)) // CHUNK
        score = jnp.einsum('bqhs,bqh->bqs',
                           jax.nn.relu(jnp.einsum('bqhd,bsd->bqhs', qib, ki)), wib)
        admissible = key_chunk[None, :] <= q_chunk[:, None]
        score = jnp.where(admissible[None], score, -jnp.inf)
        _, idx = lax.top_k(score, topk)
        k_sel = gather(k, idx)
        v_sel = gather(v, idx)
        valid = key_chunk[idx] <= q_chunk[None, :, None]
        logits = jnp.einsum('bqhd,bqkd->bqhk', qb, k_sel) * (SA_DHEAD ** -0.5)
        logits = jnp.where(valid[:, :, None, :], logits, -jnp.inf)
        p = jax.nn.softmax(logits, axis=-1)
        return jnp.einsum('bqhk,bqkd->bqhd', p, v_sel)

    o = lax.map(block, (qb_all, qib_all, wib_all, jnp.arange(nb)))
    return o.transpose(1, 0, 2, 3, 4).reshape(Bsz, S, SA_WIDTH).astype(dtype)


def hgrn2(q, f, i, g, lb, norm_g):
    dtype = q.dtype
    Bsz, S, _ = q.shape
    f32 = jnp.float32
    fg = lb + (1.0 - lb) * jax.nn.sigmoid(f.astype(f32))
    qc = to_chunks(jax.nn.silu(q.astype(f32)), HG_HEADS, HG_DHEAD)
    kc = to_chunks(1.0 - fg, HG_HEADS, HG_DHEAD)
    vc = to_chunks(i.astype(f32), HG_HEADS, HG_DHEAD)
    Bc = jnp.cumsum(to_chunks(jnp.log(fg), HG_HEADS, HG_DHEAD), axis=-2)
    q_dec = qc * jnp.exp(Bc)
    k_dec = kc * jnp.exp(Bc[..., -1:, :] - Bc)
    g_last = jnp.exp(Bc[..., -1, :])
    causal = jnp.tril(jnp.ones((CHUNK, CHUNK), bool))[..., None]

    def step(state, inp):
        q_c, k_c, v_c, b_c, qd_c, kd_c, gl_c = inp
        dec = jnp.exp(jnp.where(causal, b_c[..., :, None, :] - b_c[..., None, :, :], -jnp.inf))
        A = jnp.einsum('bhck,bhsk,bhcsk->bhcs', q_c, k_c, dec)
        o = jnp.einsum('bhck,bhkv->bhcv', qd_c, state) + jnp.einsum('bhcs,bhsv->bhcv', A, v_c)
        state = gl_c[..., :, None] * state + jnp.einsum('bhck,bhcv->bhkv', kd_c, v_c)
        return state, o

    xs = tuple(jnp.moveaxis(t, 2, 0) for t in (qc, kc, vc, Bc, q_dec, k_dec, g_last))
    s0 = jnp.zeros((Bsz, HG_HEADS, HG_DHEAD, HG_DHEAD), f32)
    _, o = lax.scan(step, s0, xs)
    o = rmsnorm(from_chunks(o), norm_g)
    o = o * jax.nn.silu(g.astype(f32).reshape(Bsz, S, HG_HEADS, HG_DHEAD))
    return o.reshape(Bsz, S, HG_WIDTH).astype(dtype)


def conv_geglu_ffn(h, w_up, conv_w, w_down):
    up = causal_depthwise_conv(h @ w_up, conv_w)
    gate, val = jnp.split(up, 2, axis=-1)
    return (jax.nn.gelu(gate, approximate=True) * val) @ w_down


def setup_inputs(seed: int = 0) -> dict:
    key = jax.random.key(seed)
    ks = jax.random.split(key, 16)
    f32 = jnp.float32
    nrm = lambda k, s: jax.random.normal(k, s, f32)
    x = nrm(ks[0], (BATCH, SEQ, D_MODEL))
    w_in = nrm(ks[1], (DEPTH, D_MODEL, D_IN)) * D_MODEL ** -0.5
    dn_conv = nrm(ks[2], (DEPTH, DN_CONV, 3 * DN_WIDTH)) * DN_CONV ** -0.5
    dn_a_log = jnp.log(jax.random.uniform(ks[3], (DEPTH, DN_HEADS), f32, 1.0, 16.0))
    dt = jnp.exp(jax.random.uniform(ks[4], (DEPTH, DN_HEADS), f32,
                                    math.log(1e-3), math.log(1e-1)))
    dn_dt_bias = dt + jnp.log(-jnp.expm1(-dt))
    dn_norm = 1.0 + 0.02 * nrm(ks[5], (DEPTH, DN_DHEAD))
    hg_lb = 0.1 * nrm(ks[6], (DEPTH, HG_WIDTH))
    hg_norm = 1.0 + 0.02 * nrm(ks[7], (DEPTH, HG_DHEAD))
    w_out = nrm(ks[8], (DEPTH, D_MIX, D_MODEL)) * D_MIX ** -0.5
    g_mix_pre = 1.0 + 0.02 * nrm(ks[9], (DEPTH, D_MODEL))
    g_mix_post = 1.0 + 0.02 * nrm(ks[10], (DEPTH, D_MODEL))
    g_ffn_pre = 1.0 + 0.02 * nrm(ks[11], (DEPTH, D_MODEL))
    g_ffn_post = 1.0 + 0.02 * nrm(ks[12], (DEPTH, D_MODEL))
    ffn_w_up = nrm(ks[13], (DEPTH, D_MODEL, 2 * D_FF)) * D_MODEL ** -0.5
    ffn_conv = nrm(ks[14], (DEPTH, FFN_CONV, 2 * D_FF)) * FFN_CONV ** -0.5
    ffn_w_down = nrm(ks[15], (DEPTH, D_FF, D_MODEL)) * D_FF ** -0.5
    return {'x': x, 'w_in': w_in, 'dn_conv': dn_conv, 'dn_a_log': dn_a_log,
            'dn_dt_bias': dn_dt_bias, 'dn_norm': dn_norm, 'hg_lb': hg_lb, 'hg_norm': hg_norm,
            'w_out': w_out, 'g_mix_pre': g_mix_pre, 'g_mix_post': g_mix_post,
            'g_ffn_pre': g_ffn_pre, 'g_ffn_post': g_ffn_post, 'ffn_w_up': ffn_w_up,
            'ffn_conv': ffn_conv, 'ffn_w_down': ffn_w_down}


def reference(x, w_in, dn_conv, dn_a_log, dn_dt_bias, dn_norm, hg_lb, hg_norm, w_out,
              g_mix_pre, g_mix_post, g_ffn_pre, g_ffn_post, ffn_w_up, ffn_conv, ffn_w_down):
    cs = jnp.cumsum(jax.nn.softmax(hg_lb.astype(jnp.float32), axis=0), axis=0)
    lower_bounds = cs - cs[0:1]
    for l in range(DEPTH):
        h = rmsnorm(x, g_mix_pre[l])
        (a_q, a_k, a_v, a_z, a_b, a_a,
         b_q, b_k, b_v, b_qi, b_ki, b_wi,
         c_q, c_f, c_i, c_g) = jnp.split(h @ w_in[l], SPLIT_POINTS, axis=-1)
        o_a = gated_deltanet(a_q, a_k, a_v, a_z, a_b, a_a, dn_conv[l], dn_a_log[l],
                             dn_dt_bias[l], dn_norm[l])
        o_b = dsa_attention(b_q, b_k, b_v, b_qi, b_ki, b_wi)
        o_c = hgrn2(c_q, c_f, c_i, c_g, lower_bounds[l], hg_norm[l])
        mix = jnp.concatenate([o_a, o_b, o_c], axis=-1) @ w_out[l]
        x = x + rmsnorm(mix, g_mix_post[l])
        h = rmsnorm(x, g_ffn_pre[l])
        x = x + rmsnorm(conv_geglu_ffn(h, ffn_w_up[l], ffn_conv[l], ffn_w_down[l]), g_ffn_post[l])
    return x
```

```python
import functools

import jax
import jax.numpy as jnp
from jax import lax
from jax.experimental import pallas as pl
from jax.experimental.pallas import tpu as pltpu

F32 = jnp.float32
BF16 = jnp.bfloat16
HIGHEST = lax.Precision.HIGHEST

LANES = 128
CHUNK = 64
RMS_EPS = 1e-6
DN_DHEAD = 128
DN_HEADS = 4
DN_WIDTH = DN_HEADS * DN_DHEAD
DN_CONV = 4
SA_DHEAD = 64
SA_HEADS = 4
SA_WIDTH = SA_HEADS * SA_DHEAD
IDX_HEADS = 4
IDX_DHEAD = 64
IDX_TOPK_MAX = 256
Q_BLOCK = 128
HG_DHEAD = 64
HG_HEADS = 4
HG_WIDTH = HG_HEADS * HG_DHEAD
HG_SUB = 16
FFN_CONV = 3
FF_TILE = 256
INT_MIN = -2 ** 31

VMEM_LIMIT = 56 * 1024 * 1024


def _mm(a, b):
    return jnp.dot(a.astype(BF16), b.astype(BF16), preferred_element_type=F32)


def _mm_nt(a, b):
    return lax.dot_general(a.astype(BF16), b.astype(BF16), (((1,), (1,)), ((), ())),
                           preferred_element_type=F32)


def _mm_f32(a, b):
    return jnp.dot(a, b, precision=HIGHEST, preferred_element_type=F32)


def _sigmoid(x):
    return 1.0 / (1.0 + jnp.exp(-x))


def _silu(x):
    return x * _sigmoid(x)


def _softplus(x):
    return jnp.maximum(x, 0.0) + jnp.log1p(jnp.exp(-jnp.abs(x)))


def _rms(x, g):
    return x * lax.rsqrt(jnp.mean(x * x, axis=-1, keepdims=True) + RMS_EPS) * g


def _proj_kernel(x_ref, g_ref, wd_ref, ws_ref, wh_ref, od_ref, os_ref, oh_ref):
    h = _rms(x_ref[...], g_ref[...]).astype(BF16)
    od_ref[...] = jnp.dot(h, wd_ref[...], preferred_element_type=F32)
    os_ref[...] = jnp.dot(h, ws_ref[...], preferred_element_type=F32)
    oh_ref[...] = jnp.dot(h, wh_ref[...], preferred_element_type=F32)


def _proj_call(x2, g, wd, ws, wh, tm=256):
    m, d = x2.shape
    full = lambda w: pl.BlockSpec(w.shape, lambda i: (0, 0))
    rows = lambda n: pl.BlockSpec((tm, n), lambda i: (i, 0))
    return pl.pallas_call(
        _proj_kernel,
        grid=(m // tm,),
        in_specs=[rows(d), full(g), full(wd), full(ws), full(wh)],
        out_specs=[rows(wd.shape[1]), rows(ws.shape[1]), rows(wh.shape[1])],
        out_shape=[jax.ShapeDtypeStruct((m, w.shape[1]), F32) for w in (wd, ws, wh)],
        compiler_params=pltpu.CompilerParams(dimension_semantics=("arbitrary",),
                                             vmem_limit_bytes=VMEM_LIMIT),
        name="proj",
    )(x2, g, wd, ws, wh)


def _dn_kernel(q_ref, k_ref, v_ref, z_ref, gt_ref, cq_ref, ck_ref, cv_ref, par_ref, ng_ref,
               o_ref):
    C = CHUNK
    n_chunks = q_ref.shape[0] // C
    h = pl.program_id(1)
    lane = lax.broadcasted_iota(jnp.int32, (C, LANES), 1)
    row = lax.broadcasted_iota(jnp.int32, (C, LANES), 0)
    r64 = lax.broadcasted_iota(jnp.int32, (C, C), 0)
    c64 = lax.broadcasted_iota(jnp.int32, (C, C), 1)
    causal = c64 <= r64
    strict = c64 < r64
    eye = c64 == r64
    tril = causal.astype(F32)
    ones = jnp.ones((C, C), F32)
    eyef = eye.astype(F32)
    neg_a = -jnp.exp(par_ref[0:1, :])
    dt_bias = par_ref[1:2, :]
    ng = ng_ref[...]
    cws = (cq_ref[...], ck_ref[...], cv_ref[...])

    def conv_silu(ref, w, r0, rp, notfirst):
        cur = ref[pl.ds(r0, C), :]
        prev = ref[pl.ds(rp, C), :] * notfirst
        acc = cur * w[DN_CONV - 1:DN_CONV, :]
        for d in range(1, DN_CONV):
            sh = jnp.where(row < d, pltpu.roll(prev, d, 0), pltpu.roll(cur, d, 0))
            acc = acc + sh * w[DN_CONV - 1 - d:DN_CONV - d, :]
        return _silu(acc)

    def l2norm(x):
        return x * lax.rsqrt(jnp.sum(x * x, axis=-1, keepdims=True) + 1e-6)

    def body(n, state):
        r0 = pl.multiple_of(n * C, C)
        rp = pl.multiple_of(jnp.maximum(n - 1, 0) * C, C)
        notfirst = (n > 0).astype(F32)
        q = l2norm(conv_silu(q_ref, cws[0], r0, rp, notfirst)) * (DN_DHEAD ** -0.5)
        k = l2norm(conv_silu(k_ref, cws[1], r0, rp, notfirst))
        v = conv_silu(v_ref, cws[2], r0, rp, notfirst)

        gt = gt_ref[pl.ds(r0, C), :]
        beta = jnp.sum(jnp.where(lane == h, _sigmoid(gt), 0.0), axis=1, keepdims=True)
        g_all = neg_a * _softplus(gt + dt_bias)
        g = jnp.sum(jnp.where(lane == DN_HEADS + h, g_all, 0.0), axis=1, keepdims=True)
        gb = _mm_f32(tril, jnp.broadcast_to(g, (C, LANES)))
        gcol = gb[:, :C]
        grow = _mm_f32(ones, jnp.where(eye, gcol, 0.0))
        decay = jnp.exp(jnp.where(causal, gcol - grow, -jnp.inf))

        kk = _mm_nt(k, k)
        x = jnp.where(strict, -(beta * kk * decay), 0.0)
        t = eyef + x
        p = x
        for _ in range(5):
            p = _mm(p, p)
            t = t + _mm(t, p)
        eg = jnp.exp(gb)
        rhs = jnp.concatenate([beta * v, (beta * eg) * k], axis=1)
        uw = _mm(t, rhs)
        u0 = uw[:, :DN_DHEAD]
        wk = uw[:, DN_DHEAD:]
        qk = _mm_nt(q, k) * decay
        q_dec = q * eg
        g_end = gb[C - 1:C, :]
        k_dec = k * jnp.exp(g_end - gb)

        sb = state.astype(BF16)
        u = u0 - _mm(wk, sb)
        o = _mm(q_dec, sb) + _mm(qk, u)
        state = state * jnp.exp(g_end) + _mm(k_dec.T, u)

        o = _rms(o, ng) * _silu(z_ref[pl.ds(r0, C), :])
        o_ref[pl.ds(r0, C), :] = o
        return state

    lax.fori_loop(0, n_chunks, body, jnp.zeros((DN_DHEAD, DN_DHEAD), F32))


def _dn_call(dn, conv_w, a_log, dt_bias, norm_g):
    b, s, _ = dn.shape
    par = jnp.zeros((8, LANES), F32)
    par = par.at[0, DN_HEADS:2 * DN_HEADS].set(a_log).at[1, DN_HEADS:2 * DN_HEADS].set(dt_bias)
    col = lambda off: pl.BlockSpec((None, s, LANES), lambda i, j, off=off: (i, 0, off + j))
    cw = lambda off: pl.BlockSpec((DN_CONV, LANES), lambda i, j, off=off: (0, off + j))
    return pl.pallas_call(
        _dn_kernel,
        grid=(b, DN_HEADS),
        in_specs=[col(0), col(4), col(8), col(12),
                  pl.BlockSpec((None, s, LANES), lambda i, j: (i, 0, 4 * DN_HEADS)),
                  cw(0), cw(4), cw(8),
                  pl.BlockSpec((8, LANES), lambda i, j: (0, 0)),
                  pl.BlockSpec((1, LANES), lambda i, j: (0, 0))],
        out_specs=pl.BlockSpec((None, s, LANES), lambda i, j: (i, 0, j)),
        out_shape=jax.ShapeDtypeStruct((b, s, DN_WIDTH), F32),
        compiler_params=pltpu.CompilerParams(dimension_semantics=("arbitrary", "arbitrary"),
                                             vmem_limit_bytes=VMEM_LIMIT),
        name="deltanet",
    )(dn, dn, dn, dn, dn, conv_w, conv_w, conv_w, par, norm_g.reshape(1, LANES))


def _hg_kernel(q_ref, f_ref, i_ref, g_ref, lb_ref, ng_ref, o_ref, *, layer):
    C = CHUNK
    n_chunks = q_ref.shape[0] // C
    nsub = C // HG_SUB
    lane = lax.broadcasted_iota(jnp.int32, (C, LANES), 1)
    row = lax.broadcasted_iota(jnp.int32, (C, LANES), 0)
    head0 = lane < HG_DHEAD
    r64 = lax.broadcasted_iota(jnp.int32, (C, C), 0)
    c64 = lax.broadcasted_iota(jnp.int32, (C, C), 1)
    tril = (c64 <= r64).astype(F32)
    rl = lax.broadcasted_iota(jnp.int32, (LANES, LANES), 0)
    cl = lax.broadcasted_iota(jnp.int32, (LANES, LANES), 1)
    same_head = (rl < HG_DHEAD) == (cl < HG_DHEAD)
    head_ones = same_head.astype(BF16)
    head_mean = same_head.astype(F32) * (1.0 / HG_DHEAD)
    sub_row = lax.broadcasted_iota(jnp.int32, (HG_SUB, LANES), 0)
    ng = ng_ref[...]

    lbr = lb_ref[...]
    e = jnp.exp(lbr - jnp.max(lbr, axis=0, keepdims=True))
    sm = e / jnp.sum(e, axis=0, keepdims=True)
    lb = jnp.zeros((1, LANES), F32)
    for i in range(1, layer + 1):
        lb = lb + sm[i:i + 1, :]

    def body(n, state_t):
        r0 = pl.multiple_of(n * C, C)
        fg = lb + (1.0 - lb) * _sigmoid(f_ref[pl.ds(r0, C), :])
        qc = _silu(q_ref[pl.ds(r0, C), :])
        kc = 1.0 - fg
        vc = i_ref[pl.ds(r0, C), :]
        bc = _mm_f32(tril, jnp.log(fg))
        b_end = bc[C - 1:C, :]
        q_dec = qc * jnp.exp(bc)
        k_dec = kc * jnp.exp(b_end - bc)

        o = _mm_nt(q_dec, state_t)

        parts = []
        for j in range(nsub):
            lo = j * HG_SUB
            bblk = bc[lo:lo + HG_SUB, :]
            qblk = qc[lo:lo + HG_SUB, :]
            for s in range(HG_SUB):
                dec = jnp.exp(jnp.where(sub_row >= s, bblk - bc[lo + s:lo + s + 1, :], -jnp.inf))
                parts.append((qblk * kc[lo + s:lo + s + 1, :] * dec).astype(BF16))
        rs = jnp.dot(jnp.concatenate(parts, axis=0), head_ones, preferred_element_type=F32)
        diag = []
        for j in range(nsub):
            lo = j * HG_SUB
            acc = jnp.zeros((HG_SUB, LANES), F32)
            for s in range(HG_SUB):
                base = (j * HG_SUB + s) * HG_SUB
                acc = acc + rs[base:base + HG_SUB, :] * vc[lo + s:lo + s + 1, :]
            diag.append(acc)
        o = o + jnp.concatenate(diag, axis=0)

        size = HG_SUB
        while size < C:
            for first in range(size, C, 2 * size):
                bref = bc[first:first + 1, :]
                qmask = (row >= first) & (row < first + size)
                kmask = (row >= first - size) & (row < first)
                qt = jnp.where(qmask, qc * jnp.exp(jnp.where(qmask, bc - bref, 0.0)), 0.0)
                kt = jnp.where(kmask, kc * jnp.exp(jnp.where(kmask, bref - bc, 0.0)), 0.0)
                a0 = _mm_nt(jnp.where(head0, qt, 0.0), kt)
                a1 = _mm_nt(jnp.where(head0, 0.0, qt), kt)
                o = o + jnp.where(head0, _mm(a0, vc), _mm(a1, vc))
            size *= 2

        state_t = jnp.where(same_head, state_t * jnp.exp(b_end) + _mm(vc.T, k_dec), 0.0)

        ms = _mm_f32(o * o, head_mean)
        o = o * lax.rsqrt(ms + RMS_EPS) * ng * _silu(g_ref[pl.ds(r0, C), :])
        o_ref[pl.ds(r0, C), :] = o
        return state_t

    lax.fori_loop(0, n_chunks, body, jnp.zeros((LANES, LANES), F32))


def _hg_call(hg, hg_lb, norm_g, layer):
    b, s, _ = hg.shape
    npair = HG_WIDTH // LANES
    depth = hg_lb.shape[0]
    col = lambda off: pl.BlockSpec((None, s, LANES), lambda i, j, off=off: (i, 0, off + j))
    ng2 = jnp.tile(norm_g, LANES // HG_DHEAD).reshape(1, LANES)
    return pl.pallas_call(
        functools.partial(_hg_kernel, layer=layer),
        grid=(b, npair),
        in_specs=[col(0), col(npair), col(2 * npair), col(3 * npair),
                  pl.BlockSpec((depth, LANES), lambda i, j: (0, j)),
                  pl.BlockSpec((1, LANES), lambda i, j: (0, 0))],
        out_specs=pl.BlockSpec((None, s, LANES), lambda i, j: (i, 0, j)),
        out_shape=jax.ShapeDtypeStruct((b, s, HG_WIDTH), F32),
        compiler_params=pltpu.CompilerParams(dimension_semantics=("arbitrary", "arbitrary"),
                                             vmem_limit_bytes=VMEM_LIMIT),
        name="hgrn2",
    )(hg, hg, hg, hg, hg_lb, ng2)


def _sa_kernel(q_ref, qi_ref, wq_ref, kv_ref, kiw_ref, o_ref, kvt_ref, key_ref, bias_ref,
               lg_ref, *, topk):
    T = Q_BLOCK
    s_len = kv_ref.shape[0]
    j = pl.program_id(1)
    nkt = j + 1
    row_k = lax.broadcasted_iota(jnp.int32, (T, T), 0)
    lane_q = lax.broadcasted_iota(jnp.int32, (T, T), 1)

    @pl.when(j == 0)
    def _():
        def tr(t, c):
            r0 = pl.multiple_of(t * T, T)
            kvt_ref[:, pl.ds(r0, T)] = kv_ref[pl.ds(r0, T), :].T
            return c
        lax.fori_loop(0, s_len // T, tr, 0)

    w_t = wq_ref[...].T * (IDX_HEADS ** -0.5 * IDX_DHEAD ** -0.5)

    def score_tile(kt, c):
        r0 = pl.multiple_of(kt * T, T)
        kib = kiw_ref[pl.ds(r0, T), :].astype(BF16)
        sc = jnp.zeros((T, T), F32)
        for hh in range(IDX_HEADS):
            d = _mm_nt(kib, qi_ref[:, hh * LANES:(hh + 1) * LANES])
            sc = sc + jnp.maximum(d, 0.0) * w_t[IDX_DHEAD + hh:IDX_DHEAD + hh + 1, :]
        sc = jnp.where(sc == 0.0, 0.0, sc)
        bits = lax.bitcast_convert_type(sc, jnp.int32)
        key = jnp.where(bits < 0, bits ^ 0x7FFFFFFF, bits)
        first_hidden_row = jnp.where(kt == j, CHUNK, T)
        inadmissible = (row_k >= first_hidden_row) & (lane_q < CHUNK)
        key_ref[pl.ds(r0, T), :] = jnp.where(inadmissible, INT_MIN, key)
        return c
    lax.fori_loop(0, nkt, score_tile, 0)

    def count(pred):
        def f(kt, acc):
            r0 = pl.multiple_of(kt * T, T)
            m = pred(key_ref[pl.ds(r0, T), :], kt * T + row_k).astype(jnp.int32)
            return acc + jnp.sum(m.reshape(T // 8, 8, T), axis=0)
        acc = lax.fori_loop(0, nkt, f, jnp.zeros((8, T), jnp.int32))
        return jnp.sum(acc, axis=0, keepdims=True)

    thr = jnp.where(count(lambda k, i: k >= 0) >= topk, 0, INT_MIN).astype(jnp.int32)

    def bit_step(i, thr):
        cand = thr + jnp.left_shift(jnp.int32(1), 30 - i)
        return jnp.where(count(lambda k, idx: k >= cand) >= topk, cand, thr)
    thr = lax.fori_loop(0, 31, bit_step, thr)

    need = topk - count(lambda k, i: k > thr)
    nbits = (s_len - 1).bit_length()

    def idx_step(i, last):
        cand = last + jnp.left_shift(jnp.int32(1), nbits - 1 - i)
        return jnp.where(count(lambda k, idx: (k == thr) & (idx < cand)) < need, cand, last)
    last = lax.fori_loop(0, nbits, idx_step, jnp.zeros((1, T), jnp.int32))

    def bias_tile(kt, c):
        r0 = pl.multiple_of(kt * T, T)
        k = key_ref[pl.ds(r0, T), :]
        sel = ((k > thr) | ((k == thr) & (kt * T + row_k <= last))) & (k != INT_MIN)
        bias_ref[pl.ds(r0, T), :] = jnp.where(sel, 0.0, -jnp.inf)
        return c
    lax.fori_loop(0, nkt, bias_tile, 0)

    for hh in range(SA_HEADS):
        qh = q_ref[:, hh * LANES:(hh + 1) * LANES].astype(BF16)

        def logits_tile(kt, m):
            r0 = pl.multiple_of(kt * T, T)
            lg = _mm_nt(kv_ref[pl.ds(r0, T), :], qh) * (SA_DHEAD ** -0.5)
            lg = lg + bias_ref[pl.ds(r0, T), :]
            lg_ref[pl.ds(r0, T), :] = lg
            return jnp.maximum(m, jnp.max(lg, axis=0, keepdims=True))
        m = lax.fori_loop(0, nkt, logits_tile, jnp.full((1, T), -jnp.inf, F32))

        def pv_tile(kt, carry):
            acc, den = carry
            r0 = pl.multiple_of(kt * T, T)
            p = jnp.exp(lg_ref[pl.ds(r0, T), :] - m)
            den = den + jnp.sum(p, axis=0, keepdims=True)
            acc = acc + _mm(kvt_ref[:, pl.ds(r0, T)], p)
            return acc, den
        acc, den = lax.fori_loop(0, nkt, pv_tile,
                                 (jnp.zeros((T, T), F32), jnp.zeros((1, T), F32)))
        res = (acc / den).T
        o_ref[:, hh * SA_DHEAD:(hh + 1) * SA_DHEAD] = res[:, SA_DHEAD:]


def _sa_call(sa):
    b, s, _ = sa.shape
    topk = min(IDX_TOPK_MAX, s // 4)
    T = Q_BLOCK
    qw = SA_HEADS * LANES
    seq = lambda blk: pl.BlockSpec((None, s, LANES), lambda i, j, blk=blk: (i, 0, blk))
    return pl.pallas_call(
        functools.partial(_sa_kernel, topk=topk),
        grid=(b, s // T),
        in_specs=[pl.BlockSpec((None, T, qw), lambda i, j: (i, j, 0)),
                  pl.BlockSpec((None, T, qw), lambda i, j: (i, j, 1)),
                  pl.BlockSpec((None, T, LANES), lambda i, j: (i, j, 2 * SA_HEADS + 1)),
                  seq(2 * SA_HEADS), seq(2 * SA_HEADS + 1)],
        out_specs=pl.BlockSpec((None, T, SA_WIDTH), lambda i, j: (i, j, 0)),
        out_shape=jax.ShapeDtypeStruct((b, s, SA_WIDTH), F32),
        scratch_shapes=[pltpu.VMEM((LANES, s), F32), pltpu.VMEM((s, T), jnp.int32),
                        pltpu.VMEM((s, T), F32), pltpu.VMEM((s, T), F32)],
        compiler_params=pltpu.CompilerParams(dimension_semantics=("arbitrary", "arbitrary"),
                                             vmem_limit_bytes=VMEM_LIMIT),
        name="dsa",
    )(sa, sa, sa, sa, sa)


def _mix_ffn_kernel(x_ref, oa_ref, ob_ref, oc_ref, wa_ref, wb_ref, wc_ref, gpost_ref, gpre_ref,
                    wup_ref, cw_ref, wdn_ref, gout_ref, o_ref, carry_ref, *, tiles_per_seq):
    tm = x_ref.shape[0]
    d_ff = wdn_ref.shape[0]
    first = pl.program_id(0) % tiles_per_seq == 0
    mix = (jnp.dot(oa_ref[...].astype(BF16), wa_ref[...], preferred_element_type=F32)
           + jnp.dot(ob_ref[...].astype(BF16), wb_ref[...], preferred_element_type=F32)
           + jnp.dot(oc_ref[...].astype(BF16), wc_ref[...], preferred_element_type=F32))
    x1 = x_ref[...] + _rms(mix, gpost_ref[...])
    h = _rms(x1, gpre_ref[...]).astype(BF16)
    row = lax.broadcasted_iota(jnp.int32, (tm, FF_TILE), 0)

    def conv(c0):
        u = jnp.dot(h, wup_ref[:, c0:c0 + FF_TILE], preferred_element_type=F32)
        prev = jnp.where(first, 0.0, carry_ref[:, c0:c0 + FF_TILE])
        carry_ref[:, c0:c0 + FF_TILE] = u[tm - 8:, :]
        u1 = jnp.where(row == 0, prev[7:8, :], pltpu.roll(u, 1, 0))
        u2 = jnp.where(row == 0, prev[6:7, :],
                       jnp.where(row == 1, prev[7:8, :], pltpu.roll(u, 2, 0)))
        w = cw_ref[:, c0:c0 + FF_TILE]
        return u2 * w[0:1, :] + u1 * w[1:2, :] + u * w[2:3, :]

    acc = jnp.zeros((tm, x_ref.shape[1]), F32)
    for c in range(d_ff // FF_TILE):
        gate = conv(c * FF_TILE)
        val = conv(d_ff + c * FF_TILE)
        inner = gate + 0.044715 * (gate * gate * gate)
        act = 0.5 * gate * (1.0 + jnp.tanh(0.7978845608028654 * inner)) * val
        acc = acc + jnp.dot(act.astype(BF16), wdn_ref[c * FF_TILE:(c + 1) * FF_TILE, :],
                            preferred_element_type=F32)
    o_ref[...] = x1 + _rms(acc, gout_ref[...])


def _mix_ffn_call(x2, oa, ob, oc, wa, wb, wc, g_post, g_pre, w_up, conv_w, w_down, g_out, seq,
                  tm=256):
    m, d = x2.shape
    rows = lambda a: pl.BlockSpec((tm, a.shape[1]), lambda i: (i, 0))
    full = lambda a: pl.BlockSpec(a.shape, lambda i: (0, 0), pipeline_mode=pl.Buffered(1))
    return pl.pallas_call(
        functools.partial(_mix_ffn_kernel, tiles_per_seq=seq // tm),
        grid=(m // tm,),
        in_specs=[rows(x2), rows(oa), rows(ob), rows(oc), full(wa), full(wb), full(wc),
                  full(g_post), full(g_pre), full(w_up), full(conv_w), full(w_down), full(g_out)],
        out_specs=pl.BlockSpec((tm, d), lambda i: (i, 0)),
        out_shape=jax.ShapeDtypeStruct((m, d), F32),
        scratch_shapes=[pltpu.VMEM((8, w_up.shape[1]), F32)],
        compiler_params=pltpu.CompilerParams(dimension_semantics=("arbitrary",),
                                             vmem_limit_bytes=VMEM_LIMIT),
        name="mix_ffn",
    )(x2, oa, ob, oc, wa, wb, wc, g_post, g_pre, w_up, conv_w, w_down, g_out)


def _split_w_in(w):
    d = w.shape[0]
    sizes = (DN_WIDTH,) * 4 + (DN_HEADS,) * 2 + (SA_WIDTH, SA_DHEAD, SA_DHEAD,
                                                IDX_HEADS * IDX_DHEAD, IDX_DHEAD, IDX_HEADS) \
        + (HG_WIDTH,) * 4
    parts = []
    off = 0
    for n in sizes:
        parts.append(w[:, off:off + n])
        off += n
    (a_q, a_k, a_v, a_z, a_b, a_a, b_q, b_k, b_v, b_qi, b_ki, b_wi, c_q, c_f, c_i, c_g) = parts
    zeros = lambda n: jnp.zeros((d, n), w.dtype)
    w_dn = jnp.concatenate([a_q, a_k, a_v, a_z, a_b, a_a, zeros(LANES - 2 * DN_HEADS)], axis=1)
    heads = lambda t, dh: [p for i in range(t.shape[1] // dh)
                           for p in (t[:, i * dh:(i + 1) * dh], zeros(LANES - dh))]
    w_sa = jnp.concatenate(heads(b_q, SA_DHEAD) + heads(b_qi, IDX_DHEAD)
                           + [b_k, b_v, b_ki, b_wi, zeros(LANES - IDX_DHEAD - IDX_HEADS)], axis=1)
    w_hg = jnp.concatenate([c_q, c_f, c_i, c_g], axis=1)
    return w_dn.astype(BF16), w_sa.astype(BF16), w_hg.astype(BF16)


def kernel(x, w_in, dn_conv, dn_a_log, dn_dt_bias, dn_norm, hg_lb, hg_norm, w_out,
           g_mix_pre, g_mix_post, g_ffn_pre, g_ffn_post, ffn_w_up, ffn_conv, ffn_w_down):
    b, s, d = x.shape
    depth = w_in.shape[0]
    x2 = x.reshape(b * s, d)
    for l in range(depth):
        w_dn, w_sa, w_hg = _split_w_in(w_in[l])
        dn, sa, hg = _proj_call(x2, g_mix_pre[l].reshape(1, d), w_dn, w_sa, w_hg)
        o_a = _dn_call(dn.reshape(b, s, -1), dn_conv[l], dn_a_log[l], dn_dt_bias[l], dn_norm[l])
        o_b = _sa_call(sa.reshape(b, s, -1))
        o_c = _hg_call(hg.reshape(b, s, -1), hg_lb, hg_norm[l], l)
        wo = w_out[l].astype(BF16)
        x2 = _mix_ffn_call(
            x2, o_a.reshape(b * s, -1), o_b.reshape(b * s, -1), o_c.reshape(b * s, -1),
            wo[:DN_WIDTH], wo[DN_WIDTH:DN_WIDTH + SA_WIDTH], wo[DN_WIDTH + SA_WIDTH:],
            g_mix_post[l].reshape(1, d), g_ffn_pre[l].reshape(1, d),
            ffn_w_up[l].astype(BF16), ffn_conv[l], ffn_w_down[l].astype(BF16),
            g_ffn_post[l].reshape(1, d), s)
    return x2.reshape(b, s, d)
```

```python
import functools

import jax
import jax.numpy as jnp
from jax import lax
from jax.experimental import pallas as pl
from jax.experimental.pallas import tpu as pltpu

F32 = jnp.float32
BF16 = jnp.bfloat16
HIGHEST = lax.Precision.HIGHEST

LANES = 128
CHUNK = 64
CHUNK_SHIFT = 6
RMS_EPS = 1e-6
DN_DHEAD = 128
DN_HEADS = 4
DN_WIDTH = DN_HEADS * DN_DHEAD
DN_CONV = 4
SA_DHEAD = 64
SA_HEADS = 4
SA_WIDTH = SA_HEADS * SA_DHEAD
IDX_HEADS = 4
IDX_DHEAD = 64
IDX_TOPK_MAX = 256
Q_BLOCK = 128
SA_KEY_TILE = 512
HG_DHEAD = 64
HG_HEADS = 4
HG_WIDTH = HG_HEADS * HG_DHEAD
HG_SUB = 16
FFN_CONV = 3
FF_TILE = 256
SEQ_TILE = 1024
INT_MIN = -2 ** 31

VMEM_LIMIT = 56 * 1024 * 1024

DN_DHEAD_SHIFT = 7
assert CHUNK == 1 << CHUNK_SHIFT and DN_DHEAD == 1 << DN_DHEAD_SHIFT


def _mm(a, b):
    return jnp.dot(a.astype(BF16), b.astype(BF16), preferred_element_type=F32)


def _mm_nt(a, b):
    return lax.dot_general(a.astype(BF16), b.astype(BF16), (((1,), (1,)), ((), ())),
                           preferred_element_type=F32)


def _mm_f32(a, b):
    return jnp.dot(a, b, precision=HIGHEST, preferred_element_type=F32)


def _sigmoid(x):
    return 1.0 / (1.0 + jnp.exp(-x))


def _silu(x):
    return x * _sigmoid(x)


def _softplus(x):
    return jnp.maximum(x, 0.0) + jnp.log1p(jnp.exp(-jnp.abs(x)))


def _rms(x, g):
    return x * lax.rsqrt(jnp.mean(x * x, axis=-1, keepdims=True) + RMS_EPS) * g


def _proj_kernel(x_ref, g_ref, wd_ref, ws_ref, wh_ref, od_ref, os_ref, oh_ref):
    h = _rms(x_ref[...], g_ref[...]).astype(BF16)
    od_ref[...] = jnp.dot(h, wd_ref[...], preferred_element_type=F32)
    os_ref[...] = jnp.dot(h, ws_ref[...], preferred_element_type=F32)
    oh_ref[...] = jnp.dot(h, wh_ref[...], preferred_element_type=F32)


def _proj_call(x2, g, wd, ws, wh, tm=256):
    m, d = x2.shape
    full = lambda w: pl.BlockSpec(w.shape, lambda i: (0, 0))
    rows = lambda n: pl.BlockSpec((tm, n), lambda i: (i, 0))
    return pl.pallas_call(
        _proj_kernel,
        grid=(m // tm,),
        in_specs=[rows(d), full(g), full(wd), full(ws), full(wh)],
        out_specs=[rows(wd.shape[1]), rows(ws.shape[1]), rows(wh.shape[1])],
        out_shape=[jax.ShapeDtypeStruct((m, w.shape[1]), F32) for w in (wd, ws, wh)],
        compiler_params=pltpu.CompilerParams(dimension_semantics=("arbitrary",),
                                             vmem_limit_bytes=VMEM_LIMIT),
        name="proj",
    )(x2, g, wd, ws, wh)


def _dn_kernel(q_ref, k_ref, v_ref, z_ref, gt_ref, cw_ref, par_ref, ng_ref, o_ref,
               state_ref, halo_ref, u0_ref, lhs_ref, qkd_ref, kdt_ref, gend_ref):
    C = CHUNK
    D = DN_DHEAD
    H = DN_HEADS
    R = H * C
    n_chunks = q_ref.shape[0] // C
    lane = lax.broadcasted_iota(jnp.int32, (C, LANES), 1)
    row8 = lax.broadcasted_iota(jnp.int32, (8, DN_WIDTH), 0)
    r64 = lax.broadcasted_iota(jnp.int32, (C, C), 0)
    c64 = lax.broadcasted_iota(jnp.int32, (C, C), 1)
    tril = (c64 <= r64).astype(F32)
    rr = lax.broadcasted_iota(jnp.int32, (R, R), 0)
    cc = lax.broadcasted_iota(jnp.int32, (R, R), 1)
    same = (rr >> CHUNK_SHIFT) == (cc >> CHUNK_SHIFT)
    causal = same & (cc <= rr)
    strict = same & (cc < rr)
    eyef = (cc == rr).astype(F32)
    own_state = (lax.broadcasted_iota(jnp.int32, (R, H * D), 0) >> CHUNK_SHIFT) == \
        (lax.broadcasted_iota(jnp.int32, (R, H * D), 1) >> DN_DHEAD_SHIFT)
    neg_a = -jnp.exp(par_ref[0:1, :])
    dt_bias = par_ref[1:2, :]
    ng = ng_ref[...]
    srcs = (q_ref, k_ref, v_ref)

    @pl.when(pl.program_id(1) == 0)
    def _():
        state_ref[...] = jnp.zeros_like(state_ref)
        halo_ref[...] = jnp.zeros_like(halo_ref)

    def conv_silu(i, r0, n):
        ref = srcs[i]
        w = cw_ref[:, i * DN_WIDTH:(i + 1) * DN_WIDTH]
        cur = ref[pl.ds(r0, C), :]
        rp = pl.multiple_of(jnp.maximum(r0 - 8, 0), 8)
        prev = jnp.where(n > 0, ref[pl.ds(rp, 8), :], halo_ref[i])
        acc = cur * w[DN_CONV - 1:DN_CONV, :]
        for d in range(1, DN_CONV):
            rolled = pltpu.roll(cur, d, 0)
            head = jnp.where(row8 < d, pltpu.roll(prev, d, 0), rolled[:8, :])
            sh = jnp.concatenate([head, rolled[8:, :]], axis=0)
            acc = acc + sh * w[DN_CONV - 1 - d:DN_CONV - d, :]
        return _silu(acc)

    def l2norm(x):
        return x * lax.rsqrt(jnp.sum(x * x, axis=-1, keepdims=True) + 1e-6)

    def stack(a):
        return jnp.concatenate([a[:, h * D:(h + 1) * D] for h in range(H)], axis=0)

    def column(a, first_lane):
        return jnp.concatenate(
            [jnp.sum(jnp.where(lane == first_lane + h, a, 0.0), axis=1, keepdims=True)
             for h in range(H)], axis=0)

    def prepare(n, carry):
        r0 = pl.multiple_of(n * C, C)
        q = l2norm(stack(conv_silu(0, r0, n))) * (D ** -0.5)
        k = l2norm(stack(conv_silu(1, r0, n)))
        v = stack(conv_silu(2, r0, n))
        gt = gt_ref[pl.ds(r0, C), :]
        beta = column(_sigmoid(gt), 0)
        g_cum = _mm_f32(tril, neg_a * _softplus(gt + dt_bias))
        gb = jnp.broadcast_to(column(g_cum, H), (R, LANES))
        gcol = jnp.concatenate([gb, gb], axis=1)
        decay = jnp.exp(jnp.where(causal, gcol - gcol.T, -jnp.inf))

        qkk = _mm_nt(jnp.concatenate([q, k], axis=0), k)
        x = jnp.where(strict, -(beta * qkk[R:, :] * decay), 0.0)
        t = eyef + x
        p = x
        for _ in range(5):
            p = _mm(p, p)
            t = t + _mm(t, p)
        eg = jnp.exp(gb)
        uw = _mm(t, jnp.concatenate([beta * v, (beta * eg) * k], axis=1))
        g_end = jnp.concatenate(
            [jnp.broadcast_to(gb[h * C + C - 1:(h + 1) * C, :], (C, LANES)) for h in range(H)],
            axis=0)
        qeg = q * eg
        u0_ref[n] = uw[:, :D]
        for h in range(H):
            rows = slice(h * C, (h + 1) * C)
            lhs_ref[n, h] = jnp.concatenate([uw[rows, D:], qeg[rows, :]], axis=0).astype(BF16)
        qkd_ref[n] = (qkk[:R, :] * decay).astype(BF16)
        kdt_ref[n] = (k * jnp.exp(g_end - gb)).T.astype(BF16)
        gend_ref[n] = jnp.concatenate(
            [jnp.broadcast_to(jnp.exp(gb[h * C + C - 1:(h + 1) * C, :]), (8, LANES))
             for h in range(H)], axis=1)
        return carry

    lax.fori_loop(0, n_chunks, prepare, 0)

    def advance(n, carry):
        r0 = pl.multiple_of(n * C, C)
        state = state_ref[...]
        ws = [jnp.dot(lhs_ref[n, h], state[:, h * D:(h + 1) * D].astype(BF16),
                      preferred_element_type=F32) for h in range(H)]
        u = u0_ref[n] - jnp.concatenate([w[:C, :] for w in ws], axis=0)
        o = jnp.concatenate([w[C:, :] for w in ws], axis=0) \
            + jnp.dot(qkd_ref[n], u.astype(BF16), preferred_element_type=F32)
        u_wide = jnp.where(own_state, jnp.concatenate([u] * H, axis=1), 0.0).astype(BF16)
        state_ref[...] = state * gend_ref[n][0:1, :] \
            + jnp.dot(kdt_ref[n], u_wide, preferred_element_type=F32)
        o = _rms(o, ng) * _silu(stack(z_ref[pl.ds(r0, C), :]))
        o_ref[pl.ds(r0, C), :] = jnp.concatenate(
            [o[h * C:(h + 1) * C, :] for h in range(H)], axis=1)
        return carry

    lax.fori_loop(0, n_chunks, advance, 0)
    last = q_ref.shape[0] - 8
    for i in range(3):
        halo_ref[i] = srcs[i][last:, :]


def _dn_call(dn, conv_w, a_log, dt_bias, norm_g):
    b, s, _ = dn.shape
    st = min(SEQ_TILE, s)
    nc = st // CHUNK
    rows = DN_HEADS * CHUNK
    par = jnp.zeros((8, LANES), F32)
    par = par.at[0, DN_HEADS:2 * DN_HEADS].set(a_log).at[1, DN_HEADS:2 * DN_HEADS].set(dt_bias)
    col = lambda blk: pl.BlockSpec((None, st, DN_WIDTH), lambda i, j, blk=blk: (i, j, blk))
    full = lambda a: pl.BlockSpec(a.shape, lambda i, j: (0, 0))
    ng = norm_g.reshape(1, LANES)
    return pl.pallas_call(
        _dn_kernel,
        grid=(b, s // st),
        in_specs=[col(0), col(1), col(2), col(3),
                  pl.BlockSpec((None, st, LANES), lambda i, j: (i, j, 4 * DN_HEADS)),
                  full(conv_w), full(par), full(ng)],
        out_specs=pl.BlockSpec((None, st, DN_WIDTH), lambda i, j: (i, j, 0)),
        out_shape=jax.ShapeDtypeStruct((b, s, DN_WIDTH), F32),
        scratch_shapes=[pltpu.VMEM((DN_DHEAD, DN_WIDTH), F32),
                        pltpu.VMEM((3, 8, DN_WIDTH), F32),
                        pltpu.VMEM((nc, rows, DN_DHEAD), F32),
                        pltpu.VMEM((nc, DN_HEADS, 2 * CHUNK, DN_DHEAD), BF16),
                        pltpu.VMEM((nc, rows, rows), BF16),
                        pltpu.VMEM((nc, DN_DHEAD, rows), BF16),
                        pltpu.VMEM((nc, 8, DN_WIDTH), F32)],
        compiler_params=pltpu.CompilerParams(dimension_semantics=("arbitrary", "arbitrary"),
                                             vmem_limit_bytes=VMEM_LIMIT),
        name="deltanet",
    )(dn, dn, dn, dn, dn, conv_w, par, ng)


def _hg_kernel(q_ref, f_ref, i_ref, g_ref, lb_ref, ng_ref, o_ref, state_ref, *, layer):
    C = CHUNK
    n_chunks = q_ref.shape[0] // C
    npair = HG_WIDTH // LANES
    nsub = C // HG_SUB
    lane = lax.broadcasted_iota(jnp.int32, (C, LANES), 1)
    row = lax.broadcasted_iota(jnp.int32, (C, LANES), 0)
    r64 = lax.broadcasted_iota(jnp.int32, (C, C), 0)
    c64 = lax.broadcasted_iota(jnp.int32, (C, C), 1)
    tril = (c64 <= r64).astype(F32)
    rl = lax.broadcasted_iota(jnp.int32, (LANES, LANES), 0)
    cl = lax.broadcasted_iota(jnp.int32, (LANES, LANES), 1)
    same_head = (rl < HG_DHEAD) == (cl < HG_DHEAD)
    head_ones = same_head.astype(BF16)
    head_mean = same_head.astype(F32) * (1.0 / HG_DHEAD)
    sub_row = lax.broadcasted_iota(jnp.int32, (HG_SUB, LANES), 0)
    pairs = []
    size = HG_SUB
    while size < C:
        pairs += [(first, size) for first in range(size, C, 2 * size)]
        size *= 2
    head0 = (lax.broadcasted_iota(jnp.int32, (C, LANES * len(pairs)), 1) & (LANES - 1)) < HG_DHEAD

    @pl.when(pl.program_id(1) == 0)
    def _():
        state_ref[...] = jnp.zeros_like(state_ref)

    lbr = lb_ref[...]
    e = jnp.exp(lbr - jnp.max(lbr, axis=0, keepdims=True))
    sm = e / jnp.sum(e, axis=0, keepdims=True)
    lb_all = jnp.zeros((1, HG_WIDTH), F32)
    for i in range(1, layer + 1):
        lb_all = lb_all + sm[i:i + 1, :]

    def body(n, carry):
        r0 = pl.multiple_of(n * C, C)
        outs = []
        for pr in range(npair):
            sl = slice(pr * LANES, (pr + 1) * LANES)
            lb = lb_all[:, sl]
            fg = lb + (1.0 - lb) * _sigmoid(f_ref[pl.ds(r0, C), sl])
            qc = _silu(q_ref[pl.ds(r0, C), sl])
            kc = 1.0 - fg
            vc = i_ref[pl.ds(r0, C), sl]
            bc = _mm_f32(tril, jnp.log(fg))
            b_end = bc[C - 1:C, :]
            q_dec = qc * jnp.exp(bc)
            k_dec = kc * jnp.exp(b_end - bc)

            state_t = state_ref[pr]
            o = _mm_nt(q_dec, state_t)

            parts = []
            for j in range(nsub):
                lo = j * HG_SUB
                bblk = bc[lo:lo + HG_SUB, :]
                qblk = qc[lo:lo + HG_SUB, :]
                for s in range(HG_SUB):
                    dec = jnp.exp(jnp.where(sub_row >= s, bblk - bc[lo + s:lo + s + 1, :],
                                            -jnp.inf))
                    parts.append((qblk * kc[lo + s:lo + s + 1, :] * dec).astype(BF16))
            rs = jnp.dot(jnp.concatenate(parts, axis=0), head_ones, preferred_element_type=F32)
            diag = []
            for j in range(nsub):
                lo = j * HG_SUB
                acc = jnp.zeros((HG_SUB, LANES), F32)
                for s in range(HG_SUB):
                    base = (j * HG_SUB + s) * HG_SUB
                    acc = acc + rs[base:base + HG_SUB, :] * vc[lo + s:lo + s + 1, :]
                diag.append(acc)
            o = o + jnp.concatenate(diag, axis=0)

            qts, kts = [], []
            for first, size in pairs:
                bref = bc[first:first + 1, :]
                qmask = (row >= first) & (row < first + size)
                kmask = (row >= first - size) & (row < first)
                qts.append(jnp.where(qmask, qc * jnp.exp(jnp.where(qmask, bc - bref, 0.0)), 0.0))
                kts.append(jnp.where(kmask, kc * jnp.exp(jnp.where(kmask, bref - bc, 0.0)), 0.0))
            qt = jnp.concatenate(qts, axis=1)
            kt = jnp.concatenate(kts, axis=1).astype(BF16)
            a0 = _mm_nt(jnp.where(head0, qt, 0.0), kt)
            a1 = _mm_nt(jnp.where(head0, 0.0, qt), kt)
            o = o + jnp.where(lane < HG_DHEAD, _mm(a0, vc), _mm(a1, vc))

            state_ref[pr] = jnp.where(same_head, state_t * jnp.exp(b_end) + _mm(vc.T, k_dec), 0.0)

            ms = _mm_f32(o * o, head_mean)
            outs.append(o * lax.rsqrt(ms + RMS_EPS) * ng_ref[...] * _silu(g_ref[pl.ds(r0, C), sl]))
        o_ref[pl.ds(r0, C), :] = jnp.concatenate(outs, axis=1)
        return carry

    lax.fori_loop(0, n_chunks, body, 0)


def _hg_call(hg, hg_lb, norm_g, layer):
    b, s, _ = hg.shape
    st = min(SEQ_TILE, s)
    npair = HG_WIDTH // LANES
    col = lambda blk: pl.BlockSpec((None, st, HG_WIDTH), lambda i, j, blk=blk: (i, j, blk))
    full = lambda a: pl.BlockSpec(a.shape, lambda i, j: (0, 0))
    ng2 = jnp.tile(norm_g, LANES // HG_DHEAD).reshape(1, LANES)
    return pl.pallas_call(
        functools.partial(_hg_kernel, layer=layer),
        grid=(b, s // st),
        in_specs=[col(0), col(1), col(2), col(3), full(hg_lb), full(ng2)],
        out_specs=pl.BlockSpec((None, st, HG_WIDTH), lambda i, j: (i, j, 0)),
        out_shape=jax.ShapeDtypeStruct((b, s, HG_WIDTH), F32),
        scratch_shapes=[pltpu.VMEM((npair, LANES, LANES), F32)],
        compiler_params=pltpu.CompilerParams(dimension_semantics=("arbitrary", "arbitrary"),
                                             vmem_limit_bytes=VMEM_LIMIT),
        name="hgrn2",
    )(hg, hg, hg, hg, hg_lb, ng2)


def _sa_kernel(q_ref, qi_ref, wq_ref, kv_ref, kiw_ref, o_ref, kvb_ref, kvt_ref, kib_ref,
               key_ref, bias_ref, acc_ref, *, topk):
    T = Q_BLOCK
    TK = SA_KEY_TILE
    NACC = 4
    s_len = kv_ref.shape[0]
    j = pl.program_id(1)
    nkt = (j * T) // TK + 1
    row_k = lax.broadcasted_iota(jnp.int32, (TK, T), 0)
    qpos = j * T + lax.broadcasted_iota(jnp.int32, (1, T), 1)
    first_hidden = ((qpos >> CHUNK_SHIFT) + 1) << CHUNK_SHIFT

    @pl.when(j == 0)
    def _():
        def prep(t, c):
            r0 = pl.multiple_of(t * TK, TK)
            kv = kv_ref[pl.ds(r0, TK), :]
            kvb_ref[pl.ds(r0, TK), :] = kv.astype(BF16)
            kvt_ref[:, pl.ds(r0, TK)] = kv.T.astype(BF16)
            kib_ref[pl.ds(r0, TK), :] = kiw_ref[pl.ds(r0, TK), :].astype(BF16)
            return c
        lax.fori_loop(0, s_len // TK, prep, 0)

    w_t = wq_ref[...].T * (IDX_HEADS ** -0.5 * IDX_DHEAD ** -0.5)
    qib = [qi_ref[:, hh * LANES:(hh + 1) * LANES].astype(BF16) for hh in range(IDX_HEADS)]

    def score_tile(kt, c):
        r0 = pl.multiple_of(kt * TK, TK)
        kib = kib_ref[pl.ds(r0, TK), :]
        sc = jnp.zeros((TK, T), F32)
        for hh in range(IDX_HEADS):
            sc = sc + jnp.maximum(_mm_nt(kib, qib[hh]), 0.0) \
                * w_t[IDX_DHEAD + hh:IDX_DHEAD + hh + 1, :]
        sc = jnp.where(sc == 0.0, 0.0, sc)
        bits = lax.bitcast_convert_type(sc, jnp.int32)
        key = jnp.where(bits < 0, bits ^ 0x7FFFFFFF, bits)
        key_ref[pl.ds(r0, TK), :] = jnp.where(kt * TK + row_k >= first_hidden, INT_MIN, key)
        return c
    lax.fori_loop(0, nkt, score_tile, 0)

    def count(pred):
        def f(kt, acc):
            r0 = pl.multiple_of(kt * TK, TK)
            m = pred(key_ref[pl.ds(r0, TK), :], kt * TK + row_k).astype(jnp.int32)
            return acc + jnp.sum(m.reshape(TK // (8 * NACC), NACC, 8, T), axis=0)
        acc = lax.fori_loop(0, nkt, f, jnp.zeros((NACC, 8, T), jnp.int32))
        return jnp.sum(jnp.sum(acc, axis=0), axis=0, keepdims=True)

    c0 = count(lambda k, i: k >= 0)
    nonneg = c0 >= topk
    thr0 = jnp.where(nonneg, 0, INT_MIN).astype(jnp.int32)
    n_gt0 = jnp.where(nonneg, 0, c0)

    def bit_step(i, carry):
        thr, n_ge, n_gt = carry
        cand = thr + jnp.left_shift(jnp.int32(1), 30 - i)
        c = count(lambda k, idx: k >= cand)
        take = c >= topk
        return jnp.where(take, cand, thr), jnp.where(take, c, n_ge), jnp.where(take, n_gt, c)
    thr, n_ge, n_gt = lax.fori_loop(0, 31, bit_step, (thr0, c0, n_gt0))

    need = topk - n_gt
    nbits = (s_len - 1).bit_length()
    has_ties = jnp.max(((n_ge > topk) & (thr > INT_MIN)).astype(jnp.int32)) > 0

    def tie_search():
        def idx_step(i, last):
            cand = last + jnp.left_shift(jnp.int32(1), nbits - 1 - i)
            c = count(lambda k, idx: (k == thr) & (idx < cand))
            return jnp.where(c < need, cand, last)
        return lax.fori_loop(0, nbits, idx_step, jnp.zeros((1, T), jnp.int32))
    last = lax.cond(has_ties, tie_search, lambda: jnp.full((1, T), s_len, jnp.int32))

    def bias_tile(kt, c):
        r0 = pl.multiple_of(kt * TK, TK)
        k = key_ref[pl.ds(r0, TK), :]
        sel = ((k > thr) | ((k == thr) & (kt * TK + row_k <= last))) & (k != INT_MIN)
        bias_ref[pl.ds(r0, TK), :] = jnp.where(sel, 0.0, -jnp.inf)
        return c
    lax.fori_loop(0, nkt, bias_tile, 0)

    acc_ref[...] = jnp.zeros_like(acc_ref)
    qhs = [q_ref[:, hh * LANES:(hh + 1) * LANES].astype(BF16) for hh in range(SA_HEADS)]

    def att_tile(kt, carry):
        r0 = pl.multiple_of(kt * TK, TK)
        kvb = kvb_ref[pl.ds(r0, TK), :]
        kvt = kvt_ref[:, pl.ds(r0, TK)]
        bias = bias_ref[pl.ds(r0, TK), :]
        out = []
        for hh in range(SA_HEADS):
            m_old, den = carry[hh]
            lg = _mm_nt(kvb, qhs[hh]) * (SA_DHEAD ** -0.5) + bias
            m_new = jnp.maximum(m_old, jnp.max(lg, axis=0, keepdims=True))
            m_use = jnp.where(m_new == -jnp.inf, 0.0, m_new)
            alpha = jnp.exp(m_old - m_use)
            p = jnp.exp(lg - m_use)
            acc_ref[hh] = acc_ref[hh] * alpha + jnp.dot(kvt, p.astype(BF16),
                                                        preferred_element_type=F32)
            out.append((m_new, den * alpha + jnp.sum(p, axis=0, keepdims=True)))
        return tuple(out)
    init = tuple((jnp.full((1, T), -jnp.inf, F32), jnp.zeros((1, T), F32))
                 for _ in range(SA_HEADS))
    fin = lax.fori_loop(0, nkt, att_tile, init)
    for hh in range(SA_HEADS):
        res = (acc_ref[hh] / fin[hh][1]).T
        o_ref[:, hh * SA_DHEAD:(hh + 1) * SA_DHEAD] = res[:, SA_DHEAD:]


def _sa_call(sa):
    b, s, _ = sa.shape
    topk = min(IDX_TOPK_MAX, s // 4)
    T = Q_BLOCK
    assert s % SA_KEY_TILE == 0
    qw = SA_HEADS * LANES
    seq = lambda blk: pl.BlockSpec((None, s, LANES), lambda i, j, blk=blk: (i, 0, blk))
    return pl.pallas_call(
        functools.partial(_sa_kernel, topk=topk),
        grid=(b, s // T),
        in_specs=[pl.BlockSpec((None, T, qw), lambda i, j: (i, j, 0)),
                  pl.BlockSpec((None, T, qw), lambda i, j: (i, j, 1)),
                  pl.BlockSpec((None, T, LANES), lambda i, j: (i, j, 2 * SA_HEADS + 1)),
                  seq(2 * SA_HEADS), seq(2 * SA_HEADS + 1)],
        out_specs=pl.BlockSpec((None, T, SA_WIDTH), lambda i, j: (i, j, 0)),
        out_shape=jax.ShapeDtypeStruct((b, s, SA_WIDTH), F32),
        scratch_shapes=[pltpu.VMEM((s, LANES), BF16), pltpu.VMEM((LANES, s), BF16),
                        pltpu.VMEM((s, LANES), BF16), pltpu.VMEM((s, T), jnp.int32),
                        pltpu.VMEM((s, T), F32), pltpu.VMEM((SA_HEADS, LANES, T), F32)],
        compiler_params=pltpu.CompilerParams(dimension_semantics=("arbitrary", "arbitrary"),
                                             vmem_limit_bytes=VMEM_LIMIT),
        name="dsa",
    )(sa, sa, sa, sa, sa)


def _mix_ffn_kernel(x_ref, oa_ref, ob_ref, oc_ref, wa_ref, wb_ref, wc_ref, gpost_ref, gpre_ref,
                    wup_ref, cw_ref, wdn_ref, gout_ref, o_ref, carry_ref, *, tiles_per_seq):
    tm = x_ref.shape[0]
    d_ff = wdn_ref.shape[0]
    first = pl.program_id(0) % tiles_per_seq == 0
    mix = (jnp.dot(oa_ref[...].astype(BF16), wa_ref[...], preferred_element_type=F32)
           + jnp.dot(ob_ref[...].astype(BF16), wb_ref[...], preferred_element_type=F32)
           + jnp.dot(oc_ref[...].astype(BF16), wc_ref[...], preferred_element_type=F32))
    x1 = x_ref[...] + _rms(mix, gpost_ref[...])
    h = _rms(x1, gpre_ref[...]).astype(BF16)
    row = lax.broadcasted_iota(jnp.int32, (tm, FF_TILE), 0)

    def conv(c0):
        u = jnp.dot(h, wup_ref[:, c0:c0 + FF_TILE], preferred_element_type=F32)
        prev = jnp.where(first, 0.0, carry_ref[:, c0:c0 + FF_TILE])
        carry_ref[:, c0:c0 + FF_TILE] = u[tm - 8:, :]
        u1 = jnp.where(row == 0, prev[7:8, :], pltpu.roll(u, 1, 0))
        u2 = jnp.where(row == 0, prev[6:7, :],
                       jnp.where(row == 1, prev[7:8, :], pltpu.roll(u, 2, 0)))
        w = cw_ref[:, c0:c0 + FF_TILE]
        return u2 * w[0:1, :] + u1 * w[1:2, :] + u * w[2:3, :]

    acc = jnp.zeros((tm, x_ref.shape[1]), F32)
    for c in range(d_ff // FF_TILE):
        gate = conv(c * FF_TILE)
        val = conv(d_ff + c * FF_TILE)
        inner = gate + 0.044715 * (gate * gate * gate)
        act = 0.5 * gate * (1.0 + jnp.tanh(0.7978845608028654 * inner)) * val
        acc = acc + jnp.dot(act.astype(BF16), wdn_ref[c * FF_TILE:(c + 1) * FF_TILE, :],
                            preferred_element_type=F32)
    o_ref[...] = x1 + _rms(acc, gout_ref[...])


def _mix_ffn_call(x2, oa, ob, oc, wa, wb, wc, g_post, g_pre, w_up, conv_w, w_down, g_out, seq,
                  tm=256):
    m, d = x2.shape
    rows = lambda a: pl.BlockSpec((tm, a.shape[1]), lambda i: (i, 0))
    full = lambda a: pl.BlockSpec(a.shape, lambda i: (0, 0), pipeline_mode=pl.Buffered(1))
    return pl.pallas_call(
        functools.partial(_mix_ffn_kernel, tiles_per_seq=seq // tm),
        grid=(m // tm,),
        in_specs=[rows(x2), rows(oa), rows(ob), rows(oc), full(wa), full(wb), full(wc),
                  full(g_post), full(g_pre), full(w_up), full(conv_w), full(w_down), full(g_out)],
        out_specs=pl.BlockSpec((tm, d), lambda i: (i, 0)),
        out_shape=jax.ShapeDtypeStruct((m, d), F32),
        scratch_shapes=[pltpu.VMEM((8, w_up.shape[1]), F32)],
        compiler_params=pltpu.CompilerParams(dimension_semantics=("arbitrary",),
                                             vmem_limit_bytes=VMEM_LIMIT),
        name="mix_ffn",
    )(x2, oa, ob, oc, wa, wb, wc, g_post, g_pre, w_up, conv_w, w_down, g_out)


def _split_w_in(w):
    d = w.shape[0]
    sizes = (DN_WIDTH,) * 4 + (DN_HEADS,) * 2 + (SA_WIDTH, SA_DHEAD, SA_DHEAD,
                                                IDX_HEADS * IDX_DHEAD, IDX_DHEAD, IDX_HEADS) \
        + (HG_WIDTH,) * 4
    parts = []
    off = 0
    for n in sizes:
        parts.append(w[:, off:off + n])
        off += n
    (a_q, a_k, a_v, a_z, a_b, a_a, b_q, b_k, b_v, b_qi, b_ki, b_wi, c_q, c_f, c_i, c_g) = parts
    zeros = lambda n: jnp.zeros((d, n), w.dtype)
    w_dn = jnp.concatenate([a_q, a_k, a_v, a_z, a_b, a_a, zeros(LANES - 2 * DN_HEADS)], axis=1)
    heads = lambda t, dh: [p for i in range(t.shape[1] // dh)
                           for p in (t[:, i * dh:(i + 1) * dh], zeros(LANES - dh))]
    w_sa = jnp.concatenate(heads(b_q, SA_DHEAD) + heads(b_qi, IDX_DHEAD)
                           + [b_k, b_v, b_ki, b_wi, zeros(LANES - IDX_DHEAD - IDX_HEADS)], axis=1)
    w_hg = jnp.concatenate([c_q, c_f, c_i, c_g], axis=1)
    return w_dn.astype(BF16), w_sa.astype(BF16), w_hg.astype(BF16)


def kernel(x, w_in, dn_conv, dn_a_log, dn_dt_bias, dn_norm, hg_lb, hg_norm, w_out,
           g_mix_pre, g_mix_post, g_ffn_pre, g_ffn_post, ffn_w_up, ffn_conv, ffn_w_down):
    b, s, d = x.shape
    depth = w_in.shape[0]
    x2 = x.reshape(b * s, d)
    for l in range(depth):
        w_dn, w_sa, w_hg = _split_w_in(w_in[l])
        dn, sa, hg = _proj_call(x2, g_mix_pre[l].reshape(1, d), w_dn, w_sa, w_hg)
        o_a = _dn_call(dn.reshape(b, s, -1), dn_conv[l], dn_a_log[l], dn_dt_bias[l], dn_norm[l])
        o_b = _sa_call(sa.reshape(b, s, -1))
        o_c = _hg_call(hg.reshape(b, s, -1), hg_lb, hg_norm[l], l)
        wo = w_out[l].astype(BF16)
        x2 = _mix_ffn_call(
            x2, o_a.reshape(b * s, -1), o_b.reshape(b * s, -1), o_c.reshape(b * s, -1),
            wo[:DN_WIDTH], wo[DN_WIDTH:DN_WIDTH + SA_WIDTH], wo[DN_WIDTH + SA_WIDTH:],
            g_mix_post[l].reshape(1, d), g_ffn_pre[l].reshape(1, d),
            ffn_w_up[l].astype(BF16), ffn_conv[l], ffn_w_down[l].astype(BF16),
            g_ffn_post[l].reshape(1, d), s)
    return x2.reshape(b, s, d)
```

```python
import functools

import jax
import jax.numpy as jnp
from jax import lax
from jax.experimental import pallas as pl
from jax.experimental.pallas import tpu as pltpu

F32 = jnp.float32
BF16 = jnp.bfloat16
HIGHEST = lax.Precision.HIGHEST

LANES = 128
CHUNK = 64
CHUNK_SHIFT = 6
RMS_EPS = 1e-6
DN_DHEAD = 128
DN_HEADS = 4
DN_WIDTH = DN_HEADS * DN_DHEAD
DN_CONV = 4
SA_DHEAD = 64
SA_HEADS = 4
SA_WIDTH = SA_HEADS * SA_DHEAD
IDX_HEADS = 4
IDX_DHEAD = 64
IDX_TOPK_MAX = 256
Q_BLOCK = 128
SA_KEY_TILE = 512
HG_DHEAD = 64
HG_HEADS = 4
HG_WIDTH = HG_HEADS * HG_DHEAD
HG_SUB = 16
FFN_CONV = 3
FF_TILE = 256
SEQ_TILE = 1024
DN_GROUP = 4
INT_MIN = -2 ** 31

VMEM_LIMIT = 56 * 1024 * 1024

DN_DHEAD_SHIFT = 7
assert CHUNK == 1 << CHUNK_SHIFT and DN_DHEAD == 1 << DN_DHEAD_SHIFT


def _mm(a, b):
    return jnp.dot(a.astype(BF16), b.astype(BF16), preferred_element_type=F32)


def _mm_nt(a, b):
    return lax.dot_general(a.astype(BF16), b.astype(BF16), (((1,), (1,)), ((), ())),
                           preferred_element_type=F32)


def _mm_f32(a, b):
    return jnp.dot(a, b, precision=HIGHEST, preferred_element_type=F32)


def _sigmoid(x):
    return 1.0 / (1.0 + jnp.exp(-x))


def _silu(x):
    return x * _sigmoid(x)


def _softplus(x):
    return jnp.maximum(x, 0.0) + jnp.log1p(jnp.exp(-jnp.abs(x)))


def _rms(x, g):
    return x * lax.rsqrt(jnp.mean(x * x, axis=-1, keepdims=True) + RMS_EPS) * g


def _proj_kernel(x_ref, g_ref, wd_ref, ws_ref, wh_ref, od_ref, os_ref, oh_ref):
    h = _rms(x_ref[...], g_ref[...]).astype(BF16)
    od_ref[...] = jnp.dot(h, wd_ref[...], preferred_element_type=F32)
    os_ref[...] = jnp.dot(h, ws_ref[...], preferred_element_type=F32)
    oh_ref[...] = jnp.dot(h, wh_ref[...], preferred_element_type=F32)


def _proj_call(x2, g, wd, ws, wh, tm=512):
    m, d = x2.shape
    full = lambda w: pl.BlockSpec(w.shape, lambda i: (0, 0))
    rows = lambda n: pl.BlockSpec((tm, n), lambda i: (i, 0))
    return pl.pallas_call(
        _proj_kernel,
        grid=(m // tm,),
        in_specs=[rows(d), full(g), full(wd), full(ws), full(wh)],
        out_specs=[rows(wd.shape[1]), rows(ws.shape[1]), rows(wh.shape[1])],
        out_shape=[jax.ShapeDtypeStruct((m, w.shape[1]), F32) for w in (wd, ws, wh)],
        compiler_params=pltpu.CompilerParams(dimension_semantics=("arbitrary",),
                                             vmem_limit_bytes=VMEM_LIMIT),
        name="proj",
    )(x2, g, wd, ws, wh)


def _dn_kernel(q_ref, k_ref, v_ref, z_ref, gt_ref, cw_ref, par_ref, ng_ref, o_ref,
               state_ref, halo_ref, u0_ref, lhs_ref, qkd_ref, kdt_ref, gend_ref):
    C = CHUNK
    D = DN_DHEAD
    H = DN_HEADS
    R = H * C
    n_chunks = q_ref.shape[0] // C
    lane = lax.broadcasted_iota(jnp.int32, (C, LANES), 1)
    row8 = lax.broadcasted_iota(jnp.int32, (8, DN_WIDTH), 0)
    r64 = lax.broadcasted_iota(jnp.int32, (C, C), 0)
    c64 = lax.broadcasted_iota(jnp.int32, (C, C), 1)
    tril = (c64 <= r64).astype(F32)
    rr = lax.broadcasted_iota(jnp.int32, (R, R), 0)
    cc = lax.broadcasted_iota(jnp.int32, (R, R), 1)
    same = (rr >> CHUNK_SHIFT) == (cc >> CHUNK_SHIFT)
    causal = same & (cc <= rr)
    strict = same & (cc < rr)
    eyef = (cc == rr).astype(F32)
    own_state = (lax.broadcasted_iota(jnp.int32, (R, H * D), 0) >> CHUNK_SHIFT) == \
        (lax.broadcasted_iota(jnp.int32, (R, H * D), 1) >> DN_DHEAD_SHIFT)
    neg_a = -jnp.exp(par_ref[0:1, :])
    dt_bias = par_ref[1:2, :]
    ng = ng_ref[...]
    srcs = (q_ref, k_ref, v_ref)

    @pl.when(pl.program_id(1) == 0)
    def _():
        state_ref[...] = jnp.zeros_like(state_ref)
        halo_ref[...] = jnp.zeros_like(halo_ref)

    def conv_silu(i, r0, n):
        ref = srcs[i]
        w = cw_ref[:, i * DN_WIDTH:(i + 1) * DN_WIDTH]
        cur = ref[pl.ds(r0, C), :]
        rp = pl.multiple_of(jnp.maximum(r0 - 8, 0), 8)
        prev = jnp.where(n > 0, ref[pl.ds(rp, 8), :], halo_ref[i])
        acc = cur * w[DN_CONV - 1:DN_CONV, :]
        for d in range(1, DN_CONV):
            rolled = pltpu.roll(cur, d, 0)
            head = jnp.where(row8 < d, pltpu.roll(prev, d, 0), rolled[:8, :])
            sh = jnp.concatenate([head, rolled[8:, :]], axis=0)
            acc = acc + sh * w[DN_CONV - 1 - d:DN_CONV - d, :]
        return _silu(acc)

    def l2norm(x):
        return x * lax.rsqrt(jnp.sum(x * x, axis=-1, keepdims=True) + 1e-6)

    def stack(a):
        return jnp.concatenate([a[:, h * D:(h + 1) * D] for h in range(H)], axis=0)

    def column(a, first_lane):
        return jnp.concatenate(
            [jnp.sum(jnp.where(lane == first_lane + h, a, 0.0), axis=1, keepdims=True)
             for h in range(H)], axis=0)

    def setup(n):
        r0 = pl.multiple_of(n * C, C)
        q = l2norm(stack(conv_silu(0, r0, n))) * (D ** -0.5)
        k = l2norm(stack(conv_silu(1, r0, n)))
        v = stack(conv_silu(2, r0, n))
        gt = gt_ref[pl.ds(r0, C), :]
        beta = column(_sigmoid(gt), 0)
        g_cum = _mm_f32(tril, neg_a * _softplus(gt + dt_bias))
        gb = jnp.broadcast_to(column(g_cum, H), (R, LANES))
        gcol = jnp.concatenate([gb, gb], axis=1)
        decay = jnp.exp(jnp.where(causal, gcol - gcol.T, -jnp.inf))

        qkk = _mm_nt(jnp.concatenate([q, k], axis=0), k)
        x = jnp.where(strict, -(beta * qkk[R:, :] * decay), 0.0)
        return x, (q, k, v, beta, gb, qkk, decay)

    def finish(n, t, aux):
        q, k, v, beta, gb, qkk, decay = aux
        eg = jnp.exp(gb)
        uw = _mm(t, jnp.concatenate([beta * v, (beta * eg) * k], axis=1))
        g_end = jnp.concatenate(
            [jnp.broadcast_to(gb[h * C + C - 1:(h + 1) * C, :], (C, LANES)) for h in range(H)],
            axis=0)
        qeg = q * eg
        u0_ref[n] = uw[:, :D]
        for h in range(H):
            rows = slice(h * C, (h + 1) * C)
            lhs_ref[n, h] = jnp.concatenate([uw[rows, D:], qeg[rows, :]], axis=0).astype(BF16)
        qkd_ref[n] = (qkk[:R, :] * decay).astype(BF16)
        kdt_ref[n] = (k * jnp.exp(g_end - gb)).T.astype(BF16)
        gend_ref[n] = jnp.concatenate(
            [jnp.broadcast_to(jnp.exp(gb[h * C + C - 1:(h + 1) * C, :]), (8, LANES))
             for h in range(H)], axis=1)

    def prepare(i, carry):
        chunks = [i * DN_GROUP + j for j in range(DN_GROUP)]
        xs, auxs = zip(*[setup(n) for n in chunks])
        ts = [eyef + x for x in xs]
        ps = list(xs)
        for _ in range(5):
            ps = [_mm(p, p) for p in ps]
            ts = [t + _mm(t, p) for t, p in zip(ts, ps)]
        for n, t, aux in zip(chunks, ts, auxs):
            finish(n, t, aux)
        return carry

    lax.fori_loop(0, n_chunks // DN_GROUP, prepare, 0)

    def advance(n, carry):
        r0 = pl.multiple_of(n * C, C)
        state = state_ref[...]
        ws = [jnp.dot(lhs_ref[n, h], state[:, h * D:(h + 1) * D].astype(BF16),
                      preferred_element_type=F32) for h in range(H)]
        u = u0_ref[n] - jnp.concatenate([w[:C, :] for w in ws], axis=0)
        o = jnp.concatenate([w[C:, :] for w in ws], axis=0) \
            + jnp.dot(qkd_ref[n], u.astype(BF16), preferred_element_type=F32)
        u_wide = jnp.where(own_state, jnp.concatenate([u] * H, axis=1), 0.0).astype(BF16)
        state_ref[...] = state * gend_ref[n][0:1, :] \
            + jnp.dot(kdt_ref[n], u_wide, preferred_element_type=F32)
        o = _rms(o, ng) * _silu(stack(z_ref[pl.ds(r0, C), :]))
        o_ref[pl.ds(r0, C), :] = jnp.concatenate(
            [o[h * C:(h + 1) * C, :] for h in range(H)], axis=1)
        return carry

    lax.fori_loop(0, n_chunks, advance, 0)
    last = q_ref.shape[0] - 8
    for i in range(3):
        halo_ref[i] = srcs[i][last:, :]


def _dn_call(dn, conv_w, a_log, dt_bias, norm_g):
    b, s, _ = dn.shape
    st = min(SEQ_TILE, s)
    nc = st // CHUNK
    rows = DN_HEADS * CHUNK
    par = jnp.zeros((8, LANES), F32)
    par = par.at[0, DN_HEADS:2 * DN_HEADS].set(a_log).at[1, DN_HEADS:2 * DN_HEADS].set(dt_bias)
    col = lambda blk: pl.BlockSpec((None, st, DN_WIDTH), lambda i, j, blk=blk: (i, j, blk))
    full = lambda a: pl.BlockSpec(a.shape, lambda i, j: (0, 0))
    ng = norm_g.reshape(1, LANES)
    return pl.pallas_call(
        _dn_kernel,
        grid=(b, s // st),
        in_specs=[col(0), col(1), col(2), col(3),
                  pl.BlockSpec((None, st, LANES), lambda i, j: (i, j, 4 * DN_HEADS)),
                  full(conv_w), full(par), full(ng)],
        out_specs=pl.BlockSpec((None, st, DN_WIDTH), lambda i, j: (i, j, 0)),
        out_shape=jax.ShapeDtypeStruct((b, s, DN_WIDTH), F32),
        scratch_shapes=[pltpu.VMEM((DN_DHEAD, DN_WIDTH), F32),
                        pltpu.VMEM((3, 8, DN_WIDTH), F32),
                        pltpu.VMEM((nc, rows, DN_DHEAD), F32),
                        pltpu.VMEM((nc, DN_HEADS, 2 * CHUNK, DN_DHEAD), BF16),
                        pltpu.VMEM((nc, rows, rows), BF16),
                        pltpu.VMEM((nc, DN_DHEAD, rows), BF16),
                        pltpu.VMEM((nc, 8, DN_WIDTH), F32)],
        compiler_params=pltpu.CompilerParams(dimension_semantics=("arbitrary", "arbitrary"),
                                             vmem_limit_bytes=VMEM_LIMIT),
        name="deltanet",
    )(dn, dn, dn, dn, dn, conv_w, par, ng)


def _hg_kernel(q_ref, f_ref, i_ref, g_ref, lb_ref, ng_ref, o_ref, state_ref, *, layer):
    C = CHUNK
    n_chunks = q_ref.shape[0] // C
    npair = HG_WIDTH // LANES
    nsub = C // HG_SUB
    lane = lax.broadcasted_iota(jnp.int32, (C, LANES), 1)
    row = lax.broadcasted_iota(jnp.int32, (C, LANES), 0)
    r64 = lax.broadcasted_iota(jnp.int32, (C, C), 0)
    c64 = lax.broadcasted_iota(jnp.int32, (C, C), 1)
    tril = (c64 <= r64).astype(F32)
    rl = lax.broadcasted_iota(jnp.int32, (LANES, LANES), 0)
    cl = lax.broadcasted_iota(jnp.int32, (LANES, LANES), 1)
    same_head = (rl < HG_DHEAD) == (cl < HG_DHEAD)
    head_ones = same_head.astype(BF16)
    head_mean = same_head.astype(F32) * (1.0 / HG_DHEAD)
    sub_row = lax.broadcasted_iota(jnp.int32, (HG_SUB, LANES), 0)
    pairs = []
    size = HG_SUB
    while size < C:
        pairs += [(first, size) for first in range(size, C, 2 * size)]
        size *= 2
    head0 = (lax.broadcasted_iota(jnp.int32, (C, LANES * len(pairs)), 1) & (LANES - 1)) < HG_DHEAD

    @pl.when(pl.program_id(1) == 0)
    def _():
        state_ref[...] = jnp.zeros_like(state_ref)

    lbr = lb_ref[...]
    e = jnp.exp(lbr - jnp.max(lbr, axis=0, keepdims=True))
    sm = e / jnp.sum(e, axis=0, keepdims=True)
    lb_all = jnp.zeros((1, HG_WIDTH), F32)
    for i in range(1, layer + 1):
        lb_all = lb_all + sm[i:i + 1, :]

    def body(n, carry):
        r0 = pl.multiple_of(n * C, C)
        outs = []
        for pr in range(npair):
            sl = slice(pr * LANES, (pr + 1) * LANES)
            lb = lb_all[:, sl]
            fg = lb + (1.0 - lb) * _sigmoid(f_ref[pl.ds(r0, C), sl])
            qc = _silu(q_ref[pl.ds(r0, C), sl])
            kc = 1.0 - fg
            vc = i_ref[pl.ds(r0, C), sl]
            bc = _mm_f32(tril, jnp.log(fg))
            b_end = bc[C - 1:C, :]
            q_dec = qc * jnp.exp(bc)
            k_dec = kc * jnp.exp(b_end - bc)

            state_t = state_ref[pr]
            o = _mm_nt(q_dec, state_t)

            parts = []
            for j in range(nsub):
                lo = j * HG_SUB
                bblk = bc[lo:lo + HG_SUB, :]
                qblk = qc[lo:lo + HG_SUB, :]
                for s in range(HG_SUB):
                    dec = jnp.exp(jnp.where(sub_row >= s, bblk - bc[lo + s:lo + s + 1, :],
                                            -jnp.inf))
                    parts.append((qblk * kc[lo + s:lo + s + 1, :] * dec).astype(BF16))
            rs = jnp.dot(jnp.concatenate(parts, axis=0), head_ones, preferred_element_type=F32)
            diag = []
            for j in range(nsub):
                lo = j * HG_SUB
                acc = jnp.zeros((HG_SUB, LANES), F32)
                for s in range(HG_SUB):
                    base = (j * HG_SUB + s) * HG_SUB
                    acc = acc + rs[base:base + HG_SUB, :] * vc[lo + s:lo + s + 1, :]
                diag.append(acc)
            o = o + jnp.concatenate(diag, axis=0)

            qts, kts = [], []
            for first, size in pairs:
                bref = bc[first:first + 1, :]
                qmask = (row >= first) & (row < first + size)
                kmask = (row >= first - size) & (row < first)
                qts.append(jnp.where(qmask, qc * jnp.exp(jnp.where(qmask, bc - bref, 0.0)), 0.0))
                kts.append(jnp.where(kmask, kc * jnp.exp(jnp.where(kmask, bref - bc, 0.0)), 0.0))
            qt = jnp.concatenate(qts, axis=1)
            kt = jnp.concatenate(kts, axis=1).astype(BF16)
            a0 = _mm_nt(jnp.where(head0, qt, 0.0), kt)
            a1 = _mm_nt(jnp.where(head0, 0.0, qt), kt)
            o = o + jnp.where(lane < HG_DHEAD, _mm(a0, vc), _mm(a1, vc))

            state_ref[pr] = jnp.where(same_head, state_t * jnp.exp(b_end) + _mm(vc.T, k_dec), 0.0)

            ms = _mm_f32(o * o, head_mean)
            outs.append(o * lax.rsqrt(ms + RMS_EPS) * ng_ref[...] * _silu(g_ref[pl.ds(r0, C), sl]))
        o_ref[pl.ds(r0, C), :] = jnp.concatenate(outs, axis=1)
        return carry

    lax.fori_loop(0, n_chunks, body, 0)


def _hg_call(hg, hg_lb, norm_g, layer):
    b, s, _ = hg.shape
    st = min(SEQ_TILE, s)
    npair = HG_WIDTH // LANES
    col = lambda blk: pl.BlockSpec((None, st, HG_WIDTH), lambda i, j, blk=blk: (i, j, blk))
    full = lambda a: pl.BlockSpec(a.shape, lambda i, j: (0, 0))
    ng2 = jnp.tile(norm_g, LANES // HG_DHEAD).reshape(1, LANES)
    return pl.pallas_call(
        functools.partial(_hg_kernel, layer=layer),
        grid=(b, s // st),
        in_specs=[col(0), col(1), col(2), col(3), full(hg_lb), full(ng2)],
        out_specs=pl.BlockSpec((None, st, HG_WIDTH), lambda i, j: (i, j, 0)),
        out_shape=jax.ShapeDtypeStruct((b, s, HG_WIDTH), F32),
        scratch_shapes=[pltpu.VMEM((npair, LANES, LANES), F32)],
        compiler_params=pltpu.CompilerParams(dimension_semantics=("arbitrary", "arbitrary"),
                                             vmem_limit_bytes=VMEM_LIMIT),
        name="hgrn2",
    )(hg, hg, hg, hg, hg_lb, ng2)


def _sa_kernel(q_ref, qi_ref, wq_ref, kv_ref, kiw_ref, o_ref, kvb_ref, kvt_ref, kib_ref,
               key_ref, bias_ref, acc_ref, tri_ref, *, topk):
    T = Q_BLOCK
    TK = SA_KEY_TILE
    NACC = 4
    s_len = kv_ref.shape[0]
    j = pl.program_id(1)
    nkt = (j * T) // TK + 1
    row_k = lax.broadcasted_iota(jnp.int32, (TK, T), 0)
    qpos = j * T + lax.broadcasted_iota(jnp.int32, (1, T), 1)
    first_hidden = ((qpos >> CHUNK_SHIFT) + 1) << CHUNK_SHIFT

    @pl.when(j == 0)
    def _():
        def prep(t, c):
            r0 = pl.multiple_of(t * TK, TK)
            kv = kv_ref[pl.ds(r0, TK), :]
            kvb_ref[pl.ds(r0, TK), :] = kv.astype(BF16)
            kvt_ref[:, pl.ds(r0, TK)] = kv.T.astype(BF16)
            kib_ref[pl.ds(r0, TK), :] = kiw_ref[pl.ds(r0, TK), :].astype(BF16)
            return c
        lax.fori_loop(0, s_len // TK, prep, 0)
        tri_ref[...] = jnp.where(lax.broadcasted_iota(jnp.int32, (TK, TK), 1)
                                 <= lax.broadcasted_iota(jnp.int32, (TK, TK), 0),
                                 1.0, 0.0).astype(BF16)

    w_t = wq_ref[...].T * (IDX_HEADS ** -0.5 * IDX_DHEAD ** -0.5)
    w_all = jnp.concatenate(
        [w_t[IDX_DHEAD + hh:IDX_DHEAD + hh + 1, :] for hh in range(IDX_HEADS)], axis=1)

    def stack_heads(ref):
        return jnp.concatenate(
            [ref[:, hh * LANES:(hh + 1) * LANES] for hh in range(ref.shape[1] // LANES)], axis=0)
    qi_all = stack_heads(qi_ref).astype(BF16)

    def score_tile(kt, c):
        r0 = pl.multiple_of(kt * TK, TK)
        d = jnp.maximum(_mm_nt(kib_ref[pl.ds(r0, TK), :], qi_all), 0.0) * w_all
        sc = d[:, :T]
        for hh in range(1, IDX_HEADS):
            sc = sc + d[:, hh * T:(hh + 1) * T]
        sc = jnp.where(sc == 0.0, 0.0, sc)
        bits = lax.bitcast_convert_type(sc, jnp.int32)
        key = jnp.where(bits < 0, bits ^ 0x7FFFFFFF, bits)
        key_ref[pl.ds(r0, TK), :] = jnp.where(kt * TK + row_k >= first_hidden, INT_MIN, key)
        return c
    lax.fori_loop(0, nkt, score_tile, 0)

    def count(pred):
        def f(kt, acc):
            r0 = pl.multiple_of(kt * TK, TK)
            m = pred(key_ref[pl.ds(r0, TK), :], kt * TK + row_k).astype(jnp.int32)
            return acc + jnp.sum(m.reshape(TK // (8 * NACC), NACC, 8, T), axis=0)
        acc = lax.fori_loop(0, nkt, f, jnp.zeros((NACC, 8, T), jnp.int32))
        return jnp.sum(jnp.sum(acc, axis=0), axis=0, keepdims=True)

    c0 = count(lambda k, i: k >= 0)
    nonneg = c0 >= topk
    thr0 = jnp.where(nonneg, 0, INT_MIN).astype(jnp.int32)
    n_gt0 = jnp.where(nonneg, 0, c0)

    def bit_step(i, carry):
        thr, n_ge, n_gt = carry
        cand = thr + jnp.left_shift(jnp.int32(1), 30 - i)
        c = count(lambda k, idx: k >= cand)
        take = c >= topk
        return jnp.where(take, cand, thr), jnp.where(take, c, n_ge), jnp.where(take, n_gt, c)
    thr, n_ge, n_gt = lax.fori_loop(0, 31, bit_step, (thr0, c0, n_gt0))

    has_ties = jnp.max(((n_ge > topk) & (thr > INT_MIN)).astype(jnp.int32)) > 0

    def bias_with_ties():
        need = jnp.where(thr > INT_MIN, topk - n_gt, 0).astype(F32)

        def bias_tile(kt, seen):
            r0 = pl.multiple_of(kt * TK, TK)
            k = key_ref[pl.ds(r0, TK), :]
            tie = k == thr
            rank = seen + jnp.dot(tri_ref[...], jnp.where(tie, 1.0, 0.0).astype(BF16),
                                  preferred_element_type=F32)
            take = jnp.where(tie, rank, jnp.inf) <= need
            bias_ref[pl.ds(r0, TK), :] = jnp.where(k > thr, 0.0, jnp.where(take, 0.0, -jnp.inf))
            return rank[TK - 1:TK, :]
        lax.fori_loop(0, nkt, bias_tile, jnp.zeros((1, T), F32))

    def bias_without_ties():
        thr_lo = jnp.maximum(thr, INT_MIN + 1)

        def bias_tile(kt, c):
            r0 = pl.multiple_of(kt * TK, TK)
            bias_ref[pl.ds(r0, TK), :] = jnp.where(key_ref[pl.ds(r0, TK), :] >= thr_lo,
                                                   0.0, -jnp.inf)
            return c
        lax.fori_loop(0, nkt, bias_tile, 0)
    lax.cond(has_ties, bias_with_ties, bias_without_ties)

    acc_ref[...] = jnp.zeros_like(acc_ref)
    q_all = (stack_heads(q_ref) * (SA_DHEAD ** -0.5)).astype(BF16)

    def att_tile(kt, carry):
        m_old, den = carry
        r0 = pl.multiple_of(kt * TK, TK)
        bias = bias_ref[pl.ds(r0, TK), :]
        lg = _mm_nt(kvb_ref[pl.ds(r0, TK), :], q_all) + jnp.concatenate([bias] * SA_HEADS, axis=1)
        m_new = jnp.maximum(m_old, jnp.max(lg, axis=0, keepdims=True))
        m_use = jnp.where(m_new == -jnp.inf, 0.0, m_new)
        alpha = jnp.exp(m_old - m_use)
        p = jnp.exp(lg - m_use)
        acc_ref[...] = acc_ref[...] * alpha + jnp.dot(kvt_ref[:, pl.ds(r0, TK)], p.astype(BF16),
                                                      preferred_element_type=F32)
        return m_new, den * alpha + jnp.sum(p, axis=0, keepdims=True)
    hq = SA_HEADS * T
    _, den = lax.fori_loop(0, nkt, att_tile,
                           (jnp.full((1, hq), -jnp.inf, F32), jnp.zeros((1, hq), F32)))
    res = acc_ref[...] / den
    for hh in range(SA_HEADS):
        o_ref[:, hh * SA_DHEAD:(hh + 1) * SA_DHEAD] = res[:, hh * T:(hh + 1) * T].T[:, SA_DHEAD:]


def _sa_call(sa):
    b, s, _ = sa.shape
    topk = min(IDX_TOPK_MAX, s // 4)
    T = Q_BLOCK
    assert s % SA_KEY_TILE == 0
    qw = SA_HEADS * LANES
    seq = lambda blk: pl.BlockSpec((None, s, LANES), lambda i, j, blk=blk: (i, 0, blk))
    return pl.pallas_call(
        functools.partial(_sa_kernel, topk=topk),
        grid=(b, s // T),
        in_specs=[pl.BlockSpec((None, T, qw), lambda i, j: (i, j, 0)),
                  pl.BlockSpec((None, T, qw), lambda i, j: (i, j, 1)),
                  pl.BlockSpec((None, T, LANES), lambda i, j: (i, j, 2 * SA_HEADS + 1)),
                  seq(2 * SA_HEADS), seq(2 * SA_HEADS + 1)],
        out_specs=pl.BlockSpec((None, T, SA_WIDTH), lambda i, j: (i, j, 0)),
        out_shape=jax.ShapeDtypeStruct((b, s, SA_WIDTH), F32),
        scratch_shapes=[pltpu.VMEM((s, LANES), BF16), pltpu.VMEM((LANES, s), BF16),
                        pltpu.VMEM((s, LANES), BF16), pltpu.VMEM((s, T), jnp.int32),
                        pltpu.VMEM((s, T), F32), pltpu.VMEM((LANES, SA_HEADS * T), F32),
                        pltpu.VMEM((SA_KEY_TILE, SA_KEY_TILE), BF16)],
        compiler_params=pltpu.CompilerParams(dimension_semantics=("arbitrary", "arbitrary"),
                                             vmem_limit_bytes=VMEM_LIMIT),
        name="dsa",
    )(sa, sa, sa, sa, sa)


def _mix_ffn_kernel(x_ref, oa_ref, ob_ref, oc_ref, wa_ref, wb_ref, wc_ref, gpost_ref, gpre_ref,
                    wup_ref, cw_ref, wdn_ref, gout_ref, o_ref, carry_ref, *, tiles_per_seq):
    tm = x_ref.shape[0]
    d_ff = wdn_ref.shape[0]
    first = pl.program_id(0) % tiles_per_seq == 0
    mix = (jnp.dot(oa_ref[...].astype(BF16), wa_ref[...], preferred_element_type=F32)
           + jnp.dot(ob_ref[...].astype(BF16), wb_ref[...], preferred_element_type=F32)
           + jnp.dot(oc_ref[...].astype(BF16), wc_ref[...], preferred_element_type=F32))
    x1 = x_ref[...] + _rms(mix, gpost_ref[...])
    h = _rms(x1, gpre_ref[...]).astype(BF16)
    row = lax.broadcasted_iota(jnp.int32, (tm, FF_TILE), 0)

    def conv(c0):
        u = jnp.dot(h, wup_ref[:, c0:c0 + FF_TILE], preferred_element_type=F32)
        prev = jnp.where(first, 0.0, carry_ref[:, c0:c0 + FF_TILE])
        carry_ref[:, c0:c0 + FF_TILE] = u[tm - 8:, :]
        u1 = jnp.where(row == 0, prev[7:8, :], pltpu.roll(u, 1, 0))
        u2 = jnp.where(row == 0, prev[6:7, :],
                       jnp.where(row == 1, prev[7:8, :], pltpu.roll(u, 2, 0)))
        w = cw_ref[:, c0:c0 + FF_TILE]
        return u2 * w[0:1, :] + u1 * w[1:2, :] + u * w[2:3, :]

    acc = jnp.zeros((tm, x_ref.shape[1]), F32)
    for c in range(d_ff // FF_TILE):
        gate = conv(c * FF_TILE)
        val = conv(d_ff + c * FF_TILE)
        inner = gate + 0.044715 * (gate * gate * gate)
        act = 0.5 * gate * (1.0 + jnp.tanh(0.7978845608028654 * inner)) * val
        acc = acc + jnp.dot(act.astype(BF16), wdn_ref[c * FF_TILE:(c + 1) * FF_TILE, :],
                            preferred_element_type=F32)
    o_ref[...] = x1 + _rms(acc, gout_ref[...])


def _mix_ffn_call(x2, oa, ob, oc, wa, wb, wc, g_post, g_pre, w_up, conv_w, w_down, g_out, seq,
                  tm=512):
    m, d = x2.shape
    rows = lambda a: pl.BlockSpec((tm, a.shape[1]), lambda i: (i, 0))
    full = lambda a: pl.BlockSpec(a.shape, lambda i: (0, 0), pipeline_mode=pl.Buffered(1))
    return pl.pallas_call(
        functools.partial(_mix_ffn_kernel, tiles_per_seq=seq // tm),
        grid=(m // tm,),
        in_specs=[rows(x2), rows(oa), rows(ob), rows(oc), full(wa), full(wb), full(wc),
                  full(g_post), full(g_pre), full(w_up), full(conv_w), full(w_down), full(g_out)],
        out_specs=pl.BlockSpec((tm, d), lambda i: (i, 0)),
        out_shape=jax.ShapeDtypeStruct((m, d), F32),
        scratch_shapes=[pltpu.VMEM((8, w_up.shape[1]), F32)],
        compiler_params=pltpu.CompilerParams(dimension_semantics=("arbitrary",),
                                             vmem_limit_bytes=VMEM_LIMIT),
        name="mix_ffn",
    )(x2, oa, ob, oc, wa, wb, wc, g_post, g_pre, w_up, conv_w, w_down, g_out)


def _split_w_in(w):
    d = w.shape[0]
    sizes = (DN_WIDTH,) * 4 + (DN_HEADS,) * 2 + (SA_WIDTH, SA_DHEAD, SA_DHEAD,
                                                IDX_HEADS * IDX_DHEAD, IDX_DHEAD, IDX_HEADS) \
        + (HG_WIDTH,) * 4
    parts = []
    off = 0
    for n in sizes:
        parts.append(w[:, off:off + n])
        off += n
    (a_q, a_k, a_v, a_z, a_b, a_a, b_q, b_k, b_v, b_qi, b_ki, b_wi, c_q, c_f, c_i, c_g) = parts
    zeros = lambda n: jnp.zeros((d, n), w.dtype)
    w_dn = jnp.concatenate([a_q, a_k, a_v, a_z, a_b, a_a, zeros(LANES - 2 * DN_HEADS)], axis=1)
    heads = lambda t, dh: [p for i in range(t.shape[1] // dh)
                           for p in (t[:, i * dh:(i + 1) * dh], zeros(LANES - dh))]
    w_sa = jnp.concatenate(heads(b_q, SA_DHEAD) + heads(b_qi, IDX_DHEAD)
                           + [b_k, b_v, b_ki, b_wi, zeros(LANES - IDX_DHEAD - IDX_HEADS)], axis=1)
    w_hg = jnp.concatenate([c_q, c_f, c_i, c_g], axis=1)
    return w_dn.astype(BF16), w_sa.astype(BF16), w_hg.astype(BF16)


def kernel(x, w_in, dn_conv, dn_a_log, dn_dt_bias, dn_norm, hg_lb, hg_norm, w_out,
           g_mix_pre, g_mix_post, g_ffn_pre, g_ffn_post, ffn_w_up, ffn_conv, ffn_w_down):
    b, s, d = x.shape
    depth = w_in.shape[0]
    x2 = x.reshape(b * s, d)
    for l in range(depth):
        w_dn, w_sa, w_hg = _split_w_in(w_in[l])
        dn, sa, hg = _proj_call(x2, g_mix_pre[l].reshape(1, d), w_dn, w_sa, w_hg)
        o_a = _dn_call(dn.reshape(b, s, -1), dn_conv[l], dn_a_log[l], dn_dt_bias[l], dn_norm[l])
        o_b = _sa_call(sa.reshape(b, s, -1))
        o_c = _hg_call(hg.reshape(b, s, -1), hg_lb, hg_norm[l], l)
        wo = w_out[l].astype(BF16)
        x2 = _mix_ffn_call(
            x2, o_a.reshape(b * s, -1), o_b.reshape(b * s, -1), o_c.reshape(b * s, -1),
            wo[:DN_WIDTH], wo[DN_WIDTH:DN_WIDTH + SA_WIDTH], wo[DN_WIDTH + SA_WIDTH:],
            g_mix_post[l].reshape(1, d), g_ffn_pre[l].reshape(1, d),
            ffn_w_up[l].astype(BF16), ffn_conv[l], ffn_w_down[l].astype(BF16),
            g_ffn_post[l].reshape(1, d), s)
    return x2.reshape(b, s, d)
```

```python
import functools

import jax
import jax.numpy as jnp
from jax import lax
from jax.experimental import pallas as pl
from jax.experimental.pallas import tpu as pltpu

F32 = jnp.float32
BF16 = jnp.bfloat16
HIGHEST = lax.Precision.HIGHEST

LANES = 128
CHUNK = 64
CHUNK_SHIFT = 6
RMS_EPS = 1e-6
DN_DHEAD = 128
DN_HEADS = 4
DN_WIDTH = DN_HEADS * DN_DHEAD
DN_CONV = 4
SA_DHEAD = 64
SA_HEADS = 4
SA_WIDTH = SA_HEADS * SA_DHEAD
IDX_HEADS = 4
IDX_DHEAD = 64
IDX_TOPK_MAX = 256
Q_BLOCK = 128
SA_KEY_TILE = 512
HG_DHEAD = 64
HG_HEADS = 4
HG_WIDTH = HG_HEADS * HG_DHEAD
HG_SUB = 16
FFN_CONV = 3
FF_TILE = 256
SEQ_TILE = 1024
DN_GROUP = 4
HG_GROUP = 4
INT_MIN = -2 ** 31

VMEM_LIMIT = 56 * 1024 * 1024

DN_DHEAD_SHIFT = 7
assert CHUNK == 1 << CHUNK_SHIFT and DN_DHEAD == 1 << DN_DHEAD_SHIFT


def _mm(a, b):
    return jnp.dot(a.astype(BF16), b.astype(BF16), preferred_element_type=F32)


def _mm_nt(a, b):
    return lax.dot_general(a.astype(BF16), b.astype(BF16), (((1,), (1,)), ((), ())),
                           preferred_element_type=F32)


def _mm_f32(a, b):
    return jnp.dot(a, b, precision=HIGHEST, preferred_element_type=F32)


def _sigmoid(x):
    return 1.0 / (1.0 + jnp.exp(-x))


def _silu(x):
    return x * _sigmoid(x)


def _softplus(x):
    return jnp.maximum(x, 0.0) + jnp.log1p(jnp.exp(-jnp.abs(x)))


def _rms(x, g):
    return x * lax.rsqrt(jnp.mean(x * x, axis=-1, keepdims=True) + RMS_EPS) * g


def _proj_kernel(x_ref, g_ref, wd_ref, ws_ref, wh_ref, od_ref, os_ref, oh_ref):
    h = _rms(x_ref[...], g_ref[...]).astype(BF16)
    od_ref[...] = jnp.dot(h, wd_ref[...], preferred_element_type=F32)
    os_ref[...] = jnp.dot(h, ws_ref[...], preferred_element_type=F32)
    oh_ref[...] = jnp.dot(h, wh_ref[...], preferred_element_type=F32)


def _proj_call(x2, g, wd, ws, wh, tm=512):
    m, d = x2.shape
    full = lambda w: pl.BlockSpec(w.shape, lambda i: (0, 0))
    rows = lambda n: pl.BlockSpec((tm, n), lambda i: (i, 0))
    return pl.pallas_call(
        _proj_kernel,
        grid=(m // tm,),
        in_specs=[rows(d), full(g), full(wd), full(ws), full(wh)],
        out_specs=[rows(wd.shape[1]), rows(ws.shape[1]), rows(wh.shape[1])],
        out_shape=[jax.ShapeDtypeStruct((m, w.shape[1]), F32) for w in (wd, ws, wh)],
        compiler_params=pltpu.CompilerParams(dimension_semantics=("arbitrary",),
                                             vmem_limit_bytes=VMEM_LIMIT),
        name="proj",
    )(x2, g, wd, ws, wh)


def _dn_kernel(q_ref, k_ref, v_ref, z_ref, gt_ref, cw_ref, par_ref, ng_ref, o_ref,
               state_ref, halo_ref, u0_ref, lhs_ref, qkd_ref, kdt_ref, gend_ref):
    C = CHUNK
    D = DN_DHEAD
    H = DN_HEADS
    R = H * C
    n_chunks = q_ref.shape[0] // C
    lane = lax.broadcasted_iota(jnp.int32, (C, LANES), 1)
    row8 = lax.broadcasted_iota(jnp.int32, (8, DN_WIDTH), 0)
    r64 = lax.broadcasted_iota(jnp.int32, (C, C), 0)
    c64 = lax.broadcasted_iota(jnp.int32, (C, C), 1)
    tril = (c64 <= r64).astype(F32)
    rr = lax.broadcasted_iota(jnp.int32, (R, R), 0)
    cc = lax.broadcasted_iota(jnp.int32, (R, R), 1)
    same = (rr >> CHUNK_SHIFT) == (cc >> CHUNK_SHIFT)
    causal = same & (cc <= rr)
    strict = same & (cc < rr)
    eyef = (cc == rr).astype(F32)
    own_state = (lax.broadcasted_iota(jnp.int32, (R, H * D), 0) >> CHUNK_SHIFT) == \
        (lax.broadcasted_iota(jnp.int32, (R, H * D), 1) >> DN_DHEAD_SHIFT)
    neg_a = -jnp.exp(par_ref[0:1, :])
    dt_bias = par_ref[1:2, :]
    ng = ng_ref[...]
    srcs = (q_ref, k_ref, v_ref)

    @pl.when(pl.program_id(1) == 0)
    def _():
        state_ref[...] = jnp.zeros_like(state_ref)
        halo_ref[...] = jnp.zeros_like(halo_ref)

    def conv_silu(i, r0, n):
        ref = srcs[i]
        w = cw_ref[:, i * DN_WIDTH:(i + 1) * DN_WIDTH]
        cur = ref[pl.ds(r0, C), :]
        rp = pl.multiple_of(jnp.maximum(r0 - 8, 0), 8)
        prev = jnp.where(n > 0, ref[pl.ds(rp, 8), :], halo_ref[i])
        acc = cur * w[DN_CONV - 1:DN_CONV, :]
        for d in range(1, DN_CONV):
            rolled = pltpu.roll(cur, d, 0)
            head = jnp.where(row8 < d, pltpu.roll(prev, d, 0), rolled[:8, :])
            sh = jnp.concatenate([head, rolled[8:, :]], axis=0)
            acc = acc + sh * w[DN_CONV - 1 - d:DN_CONV - d, :]
        return _silu(acc)

    def l2norm(x):
        return x * lax.rsqrt(jnp.sum(x * x, axis=-1, keepdims=True) + 1e-6)

    def stack(a):
        return jnp.concatenate([a[:, h * D:(h + 1) * D] for h in range(H)], axis=0)

    def column(a, first_lane):
        return jnp.concatenate(
            [jnp.sum(jnp.where(lane == first_lane + h, a, 0.0), axis=1, keepdims=True)
             for h in range(H)], axis=0)

    def setup(n):
        r0 = pl.multiple_of(n * C, C)
        q = l2norm(stack(conv_silu(0, r0, n))) * (D ** -0.5)
        k = l2norm(stack(conv_silu(1, r0, n)))
        v = stack(conv_silu(2, r0, n))
        gt = gt_ref[pl.ds(r0, C), :]
        beta = column(_sigmoid(gt), 0)
        g_cum = _mm_f32(tril, neg_a * _softplus(gt + dt_bias))
        gb = jnp.broadcast_to(column(g_cum, H), (R, LANES))
        gcol = jnp.concatenate([gb, gb], axis=1)
        decay = jnp.exp(jnp.where(causal, gcol - gcol.T, -jnp.inf))

        qkk = _mm_nt(jnp.concatenate([q, k], axis=0), k)
        x = jnp.where(strict, -(beta * qkk[R:, :] * decay), 0.0)
        return x, (q, k, v, beta, gb, qkk, decay)

    def finish(n, t, aux):
        q, k, v, beta, gb, qkk, decay = aux
        eg = jnp.exp(gb)
        uw = _mm(t, jnp.concatenate([beta * v, (beta * eg) * k], axis=1))
        g_end = jnp.concatenate(
            [jnp.broadcast_to(gb[h * C + C - 1:(h + 1) * C, :], (C, LANES)) for h in range(H)],
            axis=0)
        qeg = q * eg
        u0_ref[n] = uw[:, :D]
        for h in range(H):
            rows = slice(h * C, (h + 1) * C)
            lhs_ref[n, h] = jnp.concatenate([uw[rows, D:], qeg[rows, :]], axis=0).astype(BF16)
        qkd_ref[n] = (qkk[:R, :] * decay).astype(BF16)
        kdt_ref[n] = (k * jnp.exp(g_end - gb)).T.astype(BF16)
        gend_ref[n] = jnp.concatenate(
            [jnp.broadcast_to(jnp.exp(gb[h * C + C - 1:(h + 1) * C, :]), (8, LANES))
             for h in range(H)], axis=1)

    def prepare(i, carry):
        chunks = [i * DN_GROUP + j for j in range(DN_GROUP)]
        xs, auxs = zip(*[setup(n) for n in chunks])
        ts = [eyef + x for x in xs]
        ps = list(xs)
        for _ in range(5):
            ps = [_mm(p, p) for p in ps]
            ts = [t + _mm(t, p) for t, p in zip(ts, ps)]
        for n, t, aux in zip(chunks, ts, auxs):
            finish(n, t, aux)
        return carry

    lax.fori_loop(0, n_chunks // DN_GROUP, prepare, 0)

    def advance(n, carry):
        r0 = pl.multiple_of(n * C, C)
        state = state_ref[...]
        ws = [jnp.dot(lhs_ref[n, h], state[:, h * D:(h + 1) * D].astype(BF16),
                      preferred_element_type=F32) for h in range(H)]
        u = u0_ref[n] - jnp.concatenate([w[:C, :] for w in ws], axis=0)
        o = jnp.concatenate([w[C:, :] for w in ws], axis=0) \
            + jnp.dot(qkd_ref[n], u.astype(BF16), preferred_element_type=F32)
        u_wide = jnp.where(own_state, jnp.concatenate([u] * H, axis=1), 0.0).astype(BF16)
        state_ref[...] = state * gend_ref[n][0:1, :] \
            + jnp.dot(kdt_ref[n], u_wide, preferred_element_type=F32)
        o = _rms(o, ng) * _silu(stack(z_ref[pl.ds(r0, C), :]))
        o_ref[pl.ds(r0, C), :] = jnp.concatenate(
            [o[h * C:(h + 1) * C, :] for h in range(H)], axis=1)
        return carry

    lax.fori_loop(0, n_chunks, advance, 0, unroll=2)
    last = q_ref.shape[0] - 8
    for i in range(3):
        halo_ref[i] = srcs[i][last:, :]


def _dn_call(dn, conv_w, a_log, dt_bias, norm_g):
    b, s, _ = dn.shape
    st = min(SEQ_TILE, s)
    nc = st // CHUNK
    rows = DN_HEADS * CHUNK
    par = jnp.zeros((8, LANES), F32)
    par = par.at[0, DN_HEADS:2 * DN_HEADS].set(a_log).at[1, DN_HEADS:2 * DN_HEADS].set(dt_bias)
    col = lambda blk: pl.BlockSpec((None, st, DN_WIDTH), lambda i, j, blk=blk: (i, j, blk))
    full = lambda a: pl.BlockSpec(a.shape, lambda i, j: (0, 0))
    ng = norm_g.reshape(1, LANES)
    return pl.pallas_call(
        _dn_kernel,
        grid=(b, s // st),
        in_specs=[col(0), col(1), col(2), col(3),
                  pl.BlockSpec((None, st, LANES), lambda i, j: (i, j, 4 * DN_HEADS)),
                  full(conv_w), full(par), full(ng)],
        out_specs=pl.BlockSpec((None, st, DN_WIDTH), lambda i, j: (i, j, 0)),
        out_shape=jax.ShapeDtypeStruct((b, s, DN_WIDTH), F32),
        scratch_shapes=[pltpu.VMEM((DN_DHEAD, DN_WIDTH), F32),
                        pltpu.VMEM((3, 8, DN_WIDTH), F32),
                        pltpu.VMEM((nc, rows, DN_DHEAD), F32),
                        pltpu.VMEM((nc, DN_HEADS, 2 * CHUNK, DN_DHEAD), BF16),
                        pltpu.VMEM((nc, rows, rows), BF16),
                        pltpu.VMEM((nc, DN_DHEAD, rows), BF16),
                        pltpu.VMEM((nc, 8, DN_WIDTH), F32)],
        compiler_params=pltpu.CompilerParams(dimension_semantics=("arbitrary", "arbitrary"),
                                             vmem_limit_bytes=VMEM_LIMIT),
        name="deltanet",
    )(dn, dn, dn, dn, dn, conv_w, par, ng)


def _hg_kernel(q_ref, f_ref, i_ref, g_ref, lb_ref, ng_ref, o_ref, state_ref, intra_ref,
               qdec_ref, kdec_ref, vt_ref, gend_ref, *, layer):
    C = CHUNK
    n_chunks = q_ref.shape[0] // C
    npair = HG_WIDTH // LANES
    nsub = C // HG_SUB
    lane = lax.broadcasted_iota(jnp.int32, (C, LANES), 1)
    row = lax.broadcasted_iota(jnp.int32, (C, LANES), 0)
    r64 = lax.broadcasted_iota(jnp.int32, (C, C), 0)
    c64 = lax.broadcasted_iota(jnp.int32, (C, C), 1)
    tril = (c64 <= r64).astype(F32)
    rl = lax.broadcasted_iota(jnp.int32, (LANES, LANES), 0)
    cl = lax.broadcasted_iota(jnp.int32, (LANES, LANES), 1)
    same_head = (rl < HG_DHEAD) == (cl < HG_DHEAD)
    head_ones = same_head.astype(BF16)
    head_mean = same_head.astype(F32) * (1.0 / HG_DHEAD)
    sub_row = lax.broadcasted_iota(jnp.int32, (HG_SUB, LANES), 0)
    pairs = []
    size = HG_SUB
    while size < C:
        pairs += [(first, size) for first in range(size, C, 2 * size)]
        size *= 2
    head0 = (lax.broadcasted_iota(jnp.int32, (C, LANES * len(pairs)), 1) & (LANES - 1)) < HG_DHEAD

    @pl.when(pl.program_id(1) == 0)
    def _():
        state_ref[...] = jnp.zeros_like(state_ref)

    lbr = lb_ref[...]
    e = jnp.exp(lbr - jnp.max(lbr, axis=0, keepdims=True))
    sm = e / jnp.sum(e, axis=0, keepdims=True)
    lb_all = jnp.zeros((1, HG_WIDTH), F32)
    for i in range(1, layer + 1):
        lb_all = lb_all + sm[i:i + 1, :]

    def load(unit):
        n, pr = unit
        r0 = pl.multiple_of(n * C, C)
        sl = slice(pr * LANES, (pr + 1) * LANES)
        lb = lb_all[:, sl]
        fg = lb + (1.0 - lb) * _sigmoid(f_ref[pl.ds(r0, C), sl])
        qc = _silu(q_ref[pl.ds(r0, C), sl])
        vc = i_ref[pl.ds(r0, C), sl]
        bc = _mm_f32(tril, jnp.log(fg))
        return qc, 1.0 - fg, vc, bc

    def diag_products(vals):
        qc, kc, vc, bc = vals
        parts = []
        for j in range(nsub):
            lo = j * HG_SUB
            bblk = bc[lo:lo + HG_SUB, :]
            qblk = qc[lo:lo + HG_SUB, :]
            for s in range(HG_SUB):
                dec = jnp.exp(jnp.where(sub_row >= s, bblk - bc[lo + s:lo + s + 1, :], -jnp.inf))
                parts.append((qblk * kc[lo + s:lo + s + 1, :] * dec).astype(BF16))
        return jnp.dot(jnp.concatenate(parts, axis=0), head_ones, preferred_element_type=F32)

    def off_diag_scores(vals):
        qc, kc, vc, bc = vals
        qts, kts = [], []
        for first, size in pairs:
            bref = bc[first:first + 1, :]
            qmask = (row >= first) & (row < first + size)
            kmask = (row >= first - size) & (row < first)
            qts.append(jnp.where(qmask, qc * jnp.exp(jnp.where(qmask, bc - bref, 0.0)), 0.0))
            kts.append(jnp.where(kmask, kc * jnp.exp(jnp.where(kmask, bref - bc, 0.0)), 0.0))
        qt = jnp.concatenate(qts, axis=1)
        kt = jnp.concatenate(kts, axis=1).astype(BF16)
        return _mm_nt(jnp.where(head0, qt, 0.0), kt), _mm_nt(jnp.where(head0, 0.0, qt), kt)

    def store(unit, vals, rs, a01):
        n, pr = unit
        qc, kc, vc, bc = vals
        diag = []
        for j in range(nsub):
            lo = j * HG_SUB
            acc = jnp.zeros((HG_SUB, LANES), F32)
            for s in range(HG_SUB):
                base = (j * HG_SUB + s) * HG_SUB
                acc = acc + rs[base:base + HG_SUB, :] * vc[lo + s:lo + s + 1, :]
            diag.append(acc)
        b_end = bc[C - 1:C, :]
        intra_ref[n, pr] = jnp.concatenate(diag, axis=0) \
            + jnp.where(lane < HG_DHEAD, _mm(a01[0], vc), _mm(a01[1], vc))
        qdec_ref[n, pr] = (qc * jnp.exp(bc)).astype(BF16)
        kdec_ref[n, pr] = (kc * jnp.exp(b_end - bc)).astype(BF16)
        vt_ref[n, pr] = vc.T.astype(BF16)
        gend_ref[n, pr] = jnp.broadcast_to(jnp.exp(b_end), (8, LANES))

    def prepare(i, carry):
        units = [(i * HG_GROUP + c, pr) for c in range(HG_GROUP) for pr in range(npair)]
        vals = [load(u) for u in units]
        rss = [diag_products(v) for v in vals]
        a01s = [off_diag_scores(v) for v in vals]
        for u, v, rs, a01 in zip(units, vals, rss, a01s):
            store(u, v, rs, a01)
        return carry

    lax.fori_loop(0, n_chunks // HG_GROUP, prepare, 0)

    def advance(i, carry):
        chunks = [i * HG_GROUP + c for c in range(HG_GROUP)]
        states = []
        for pr in range(npair):
            seq = [state_ref[pr]]
            for n in chunks:
                seq.append(jnp.where(
                    same_head,
                    seq[-1] * gend_ref[n, pr][0:1, :]
                    + jnp.dot(vt_ref[n, pr], kdec_ref[n, pr], preferred_element_type=F32), 0.0))
            state_ref[pr] = seq[-1]
            states.append(seq)
        os = [[intra_ref[n, pr] + lax.dot_general(
            qdec_ref[n, pr], states[pr][c].astype(BF16), (((1,), (1,)), ((), ())),
            preferred_element_type=F32) for pr in range(npair)] for c, n in enumerate(chunks)]
        mss = [[_mm_f32(o * o, head_mean) for o in row_os] for row_os in os]
        for c, n in enumerate(chunks):
            r0 = pl.multiple_of(n * C, C)
            o_ref[pl.ds(r0, C), :] = jnp.concatenate(
                [os[c][pr] * lax.rsqrt(mss[c][pr] + RMS_EPS) * ng_ref[...]
                 * _silu(g_ref[pl.ds(r0, C), pr * LANES:(pr + 1) * LANES])
                 for pr in range(npair)], axis=1)
        return carry

    lax.fori_loop(0, n_chunks // HG_GROUP, advance, 0)


def _hg_call(hg, hg_lb, norm_g, layer):
    b, s, _ = hg.shape
    st = min(SEQ_TILE, s)
    nc = st // CHUNK
    npair = HG_WIDTH // LANES
    col = lambda blk: pl.BlockSpec((None, st, HG_WIDTH), lambda i, j, blk=blk: (i, j, blk))
    full = lambda a: pl.BlockSpec(a.shape, lambda i, j: (0, 0))
    ng2 = jnp.tile(norm_g, LANES // HG_DHEAD).reshape(1, LANES)
    return pl.pallas_call(
        functools.partial(_hg_kernel, layer=layer),
        grid=(b, s // st),
        in_specs=[col(0), col(1), col(2), col(3), full(hg_lb), full(ng2)],
        out_specs=pl.BlockSpec((None, st, HG_WIDTH), lambda i, j: (i, j, 0)),
        out_shape=jax.ShapeDtypeStruct((b, s, HG_WIDTH), F32),
        scratch_shapes=[pltpu.VMEM((npair, LANES, LANES), F32),
                        pltpu.VMEM((nc, npair, CHUNK, LANES), F32),
                        pltpu.VMEM((nc, npair, CHUNK, LANES), BF16),
                        pltpu.VMEM((nc, npair, CHUNK, LANES), BF16),
                        pltpu.VMEM((nc, npair, LANES, CHUNK), BF16),
                        pltpu.VMEM((nc, npair, 8, LANES), F32)],
        compiler_params=pltpu.CompilerParams(dimension_semantics=("arbitrary", "arbitrary"),
                                             vmem_limit_bytes=VMEM_LIMIT),
        name="hgrn2",
    )(hg, hg, hg, hg, hg_lb, ng2)


def _sa_kernel(q_ref, qi_ref, wq_ref, kv_ref, kiw_ref, o_ref, kvb_ref, kvt_ref, kib_ref,
               key_ref, bias_ref, acc_ref, tri_ref, *, topk):
    T = Q_BLOCK
    TK = SA_KEY_TILE
    NACC = 4
    s_len = kv_ref.shape[0]
    j = pl.program_id(1)
    nkt = (j * T) // TK + 1
    row_k = lax.broadcasted_iota(jnp.int32, (TK, T), 0)
    qpos = j * T + lax.broadcasted_iota(jnp.int32, (1, T), 1)
    first_hidden = ((qpos >> CHUNK_SHIFT) + 1) << CHUNK_SHIFT

    @pl.when(j == 0)
    def _():
        def prep(t, c):
            r0 = pl.multiple_of(t * TK, TK)
            kv = kv_ref[pl.ds(r0, TK), :]
            kvb_ref[pl.ds(r0, TK), :] = kv.astype(BF16)
            kvt_ref[:, pl.ds(r0, TK)] = kv.T.astype(BF16)
            kib_ref[pl.ds(r0, TK), :] = kiw_ref[pl.ds(r0, TK), :].astype(BF16)
            return c
        lax.fori_loop(0, s_len // TK, prep, 0)
        tri_ref[...] = jnp.where(lax.broadcasted_iota(jnp.int32, (TK, TK), 1)
                                 <= lax.broadcasted_iota(jnp.int32, (TK, TK), 0),
                                 1.0, 0.0).astype(BF16)

    w_t = wq_ref[...].T * (IDX_HEADS ** -0.5 * IDX_DHEAD ** -0.5)
    w_all = jnp.concatenate(
        [w_t[IDX_DHEAD + hh:IDX_DHEAD + hh + 1, :] for hh in range(IDX_HEADS)], axis=1)

    def stack_heads(ref):
        return jnp.concatenate(
            [ref[:, hh * LANES:(hh + 1) * LANES] for hh in range(ref.shape[1] // LANES)], axis=0)
    qi_all = stack_heads(qi_ref).astype(BF16)

    n_pairs = (nkt + 1) // 2

    def score_pair(i, c):
        kts = [2 * i, 2 * i + 1]
        r0s = [pl.multiple_of(kt * TK, TK) for kt in kts]
        ds = [_mm_nt(kib_ref[pl.ds(r0, TK), :], qi_all) for r0 in r0s]
        for kt, r0, d in zip(kts, r0s, ds):
            d = jnp.maximum(d, 0.0) * w_all
            sc = d[:, :T]
            for hh in range(1, IDX_HEADS):
                sc = sc + d[:, hh * T:(hh + 1) * T]
            sc = jnp.where(sc == 0.0, 0.0, sc)
            bits = lax.bitcast_convert_type(sc, jnp.int32)
            key = jnp.where(bits < 0, bits ^ 0x7FFFFFFF, bits)
            key_ref[pl.ds(r0, TK), :] = jnp.where(kt * TK + row_k >= first_hidden, INT_MIN, key)
        return c
    lax.fori_loop(0, n_pairs, score_pair, 0)

    def count(pred):
        def f(kt, acc):
            r0 = pl.multiple_of(kt * TK, TK)
            m = pred(key_ref[pl.ds(r0, TK), :], kt * TK + row_k).astype(jnp.int32)
            return acc + jnp.sum(m.reshape(TK // (8 * NACC), NACC, 8, T), axis=0)
        acc = lax.fori_loop(0, nkt, f, jnp.zeros((NACC, 8, T), jnp.int32))
        return jnp.sum(jnp.sum(acc, axis=0), axis=0, keepdims=True)

    c0 = count(lambda k, i: k >= 0)
    nonneg = c0 >= topk
    thr0 = jnp.where(nonneg, 0, INT_MIN).astype(jnp.int32)
    n_gt0 = jnp.where(nonneg, 0, c0)

    def bit_step(i, carry):
        thr, n_ge, n_gt = carry
        cand = thr + jnp.left_shift(jnp.int32(1), 30 - i)
        c = count(lambda k, idx: k >= cand)
        take = c >= topk
        return jnp.where(take, cand, thr), jnp.where(take, c, n_ge), jnp.where(take, n_gt, c)
    thr, n_ge, n_gt = lax.fori_loop(0, 31, bit_step, (thr0, c0, n_gt0))

    has_ties = jnp.max(((n_ge > topk) & (thr > INT_MIN)).astype(jnp.int32)) > 0

    def bias_with_ties():
        need = jnp.where(thr > INT_MIN, topk - n_gt, 0).astype(F32)

        def bias_tile(kt, seen):
            r0 = pl.multiple_of(kt * TK, TK)
            k = key_ref[pl.ds(r0, TK), :]
            tie = k == thr
            rank = seen + jnp.dot(tri_ref[...], jnp.where(tie, 1.0, 0.0).astype(BF16),
                                  preferred_element_type=F32)
            take = jnp.where(tie, rank, jnp.inf) <= need
            bias_ref[pl.ds(r0, TK), :] = jnp.where(k > thr, 0.0, jnp.where(take, 0.0, -jnp.inf))
            return rank[TK - 1:TK, :]
        lax.fori_loop(0, 2 * n_pairs, bias_tile, jnp.zeros((1, T), F32))

    def bias_without_ties():
        thr_lo = jnp.maximum(thr, INT_MIN + 1)

        def bias_tile(kt, c):
            r0 = pl.multiple_of(kt * TK, TK)
            bias_ref[pl.ds(r0, TK), :] = jnp.where(key_ref[pl.ds(r0, TK), :] >= thr_lo,
                                                   0.0, -jnp.inf)
            return c
        lax.fori_loop(0, 2 * n_pairs, bias_tile, 0)
    lax.cond(has_ties, bias_with_ties, bias_without_ties)

    acc_ref[...] = jnp.zeros_like(acc_ref)
    q_all = (stack_heads(q_ref) * (SA_DHEAD ** -0.5)).astype(BF16)

    def att_pair(i, carry):
        m_old, den = carry
        r0s = [pl.multiple_of((2 * i + t) * TK, TK) for t in range(2)]
        lgs = [_mm_nt(kvb_ref[pl.ds(r0, TK), :], q_all) for r0 in r0s]
        lgs = [lg + jnp.concatenate([bias_ref[pl.ds(r0, TK), :]] * SA_HEADS, axis=1)
               for lg, r0 in zip(lgs, r0s)]
        m_new = jnp.maximum(m_old, jnp.max(jnp.maximum(lgs[0], lgs[1]), axis=0, keepdims=True))
        m_use = jnp.where(m_new == -jnp.inf, 0.0, m_new)
        alpha = jnp.exp(m_old - m_use)
        ps = [jnp.exp(lg - m_use) for lg in lgs]
        pvs = [jnp.dot(kvt_ref[:, pl.ds(r0, TK)], p.astype(BF16), preferred_element_type=F32)
               for p, r0 in zip(ps, r0s)]
        acc_ref[...] = acc_ref[...] * alpha + (pvs[0] + pvs[1])
        return m_new, den * alpha + jnp.sum(ps[0] + ps[1], axis=0, keepdims=True)
    hq = SA_HEADS * T
    _, den = lax.fori_loop(0, n_pairs, att_pair,
                           (jnp.full((1, hq), -jnp.inf, F32), jnp.zeros((1, hq), F32)))
    res = acc_ref[...] / den
    for hh in range(SA_HEADS):
        o_ref[:, hh * SA_DHEAD:(hh + 1) * SA_DHEAD] = res[:, hh * T:(hh + 1) * T].T[:, SA_DHEAD:]


def _sa_call(sa):
    b, s, _ = sa.shape
    topk = min(IDX_TOPK_MAX, s // 4)
    T = Q_BLOCK
    assert s % (2 * SA_KEY_TILE) == 0
    qw = SA_HEADS * LANES
    seq = lambda blk: pl.BlockSpec((None, s, LANES), lambda i, j, blk=blk: (i, 0, blk))
    return pl.pallas_call(
        functools.partial(_sa_kernel, topk=topk),
        grid=(b, s // T),
        in_specs=[pl.BlockSpec((None, T, qw), lambda i, j: (i, j, 0)),
                  pl.BlockSpec((None, T, qw), lambda i, j: (i, j, 1)),
                  pl.BlockSpec((None, T, LANES), lambda i, j: (i, j, 2 * SA_HEADS + 1)),
                  seq(2 * SA_HEADS), seq(2 * SA_HEADS + 1)],
        out_specs=pl.BlockSpec((None, T, SA_WIDTH), lambda i, j: (i, j, 0)),
        out_shape=jax.ShapeDtypeStruct((b, s, SA_WIDTH), F32),
        scratch_shapes=[pltpu.VMEM((s, LANES), BF16), pltpu.VMEM((LANES, s), BF16),
                        pltpu.VMEM((s, LANES), BF16), pltpu.VMEM((s, T), jnp.int32),
                        pltpu.VMEM((s, T), F32), pltpu.VMEM((LANES, SA_HEADS * T), F32),
                        pltpu.VMEM((SA_KEY_TILE, SA_KEY_TILE), BF16)],
        compiler_params=pltpu.CompilerParams(dimension_semantics=("arbitrary", "arbitrary"),
                                             vmem_limit_bytes=VMEM_LIMIT),
        name="dsa",
    )(sa, sa, sa, sa, sa)


def _mix_ffn_kernel(x_ref, oa_ref, ob_ref, oc_ref, wo_ref, gpost_ref, gpre_ref,
                    wup_ref, cw_ref, wdn_ref, gout_ref, o_ref, carry_ref, act_ref, *,
                    tiles_per_seq):
    tm = x_ref.shape[0]
    d_ff = wdn_ref.shape[0]
    first = pl.program_id(0) % tiles_per_seq == 0
    mixed = jnp.concatenate([oa_ref[...], ob_ref[...], oc_ref[...]], axis=1).astype(BF16)
    mix = jnp.dot(mixed, wo_ref[...], preferred_element_type=F32)
    x1 = x_ref[...] + _rms(mix, gpost_ref[...])
    h = _rms(x1, gpre_ref[...]).astype(BF16)
    row = lax.broadcasted_iota(jnp.int32, (tm, FF_TILE), 0)

    def conv(c0):
        u = jnp.dot(h, wup_ref[:, c0:c0 + FF_TILE], preferred_element_type=F32)
        prev = jnp.where(first, 0.0, carry_ref[:, c0:c0 + FF_TILE])
        carry_ref[:, c0:c0 + FF_TILE] = u[tm - 8:, :]
        u1 = jnp.where(row == 0, prev[7:8, :], pltpu.roll(u, 1, 0))
        u2 = jnp.where(row == 0, prev[6:7, :],
                       jnp.where(row == 1, prev[7:8, :], pltpu.roll(u, 2, 0)))
        w = cw_ref[:, c0:c0 + FF_TILE]
        return u2 * w[0:1, :] + u1 * w[1:2, :] + u * w[2:3, :]

    for c in range(d_ff // FF_TILE):
        gate = conv(c * FF_TILE)
        val = conv(d_ff + c * FF_TILE)
        inner = gate + 0.044715 * (gate * gate * gate)
        act = 0.5 * gate * (1.0 + jnp.tanh(0.7978845608028654 * inner)) * val
        act_ref[:, c * FF_TILE:(c + 1) * FF_TILE] = act.astype(BF16)
    down = jnp.dot(act_ref[...], wdn_ref[...], preferred_element_type=F32)
    o_ref[...] = x1 + _rms(down, gout_ref[...])


def _mix_ffn_call(x2, oa, ob, oc, wo, g_post, g_pre, w_up, conv_w, w_down, g_out, seq,
                  tm=512):
    m, d = x2.shape
    rows = lambda a: pl.BlockSpec((tm, a.shape[1]), lambda i: (i, 0))
    full = lambda a: pl.BlockSpec(a.shape, lambda i: (0, 0), pipeline_mode=pl.Buffered(1))
    return pl.pallas_call(
        functools.partial(_mix_ffn_kernel, tiles_per_seq=seq // tm),
        grid=(m // tm,),
        in_specs=[rows(x2), rows(oa), rows(ob), rows(oc), full(wo),
                  full(g_post), full(g_pre), full(w_up), full(conv_w), full(w_down), full(g_out)],
        out_specs=pl.BlockSpec((tm, d), lambda i: (i, 0)),
        out_shape=jax.ShapeDtypeStruct((m, d), F32),
        scratch_shapes=[pltpu.VMEM((8, w_up.shape[1]), F32),
                        pltpu.VMEM((tm, w_down.shape[0]), BF16)],
        compiler_params=pltpu.CompilerParams(dimension_semantics=("arbitrary",),
                                             vmem_limit_bytes=VMEM_LIMIT),
        name="mix_ffn",
    )(x2, oa, ob, oc, wo, g_post, g_pre, w_up, conv_w, w_down, g_out)


def _split_w_in(w):
    d = w.shape[0]
    sizes = (DN_WIDTH,) * 4 + (DN_HEADS,) * 2 + (SA_WIDTH, SA_DHEAD, SA_DHEAD,
                                                IDX_HEADS * IDX_DHEAD, IDX_DHEAD, IDX_HEADS) \
        + (HG_WIDTH,) * 4
    parts = []
    off = 0
    for n in sizes:
        parts.append(w[:, off:off + n])
        off += n
    (a_q, a_k, a_v, a_z, a_b, a_a, b_q, b_k, b_v, b_qi, b_ki, b_wi, c_q, c_f, c_i, c_g) = parts
    zeros = lambda n: jnp.zeros((d, n), w.dtype)
    w_dn = jnp.concatenate([a_q, a_k, a_v, a_z, a_b, a_a, zeros(LANES - 2 * DN_HEADS)], axis=1)
    heads = lambda t, dh: [p for i in range(t.shape[1] // dh)
                           for p in (t[:, i * dh:(i + 1) * dh], zeros(LANES - dh))]
    w_sa = jnp.concatenate(heads(b_q, SA_DHEAD) + heads(b_qi, IDX_DHEAD)
                           + [b_k, b_v, b_ki, b_wi, zeros(LANES - IDX_DHEAD - IDX_HEADS)], axis=1)
    w_hg = jnp.concatenate([c_q, c_f, c_i, c_g], axis=1)
    return w_dn.astype(BF16), w_sa.astype(BF16), w_hg.astype(BF16)


def kernel(x, w_in, dn_conv, dn_a_log, dn_dt_bias, dn_norm, hg_lb, hg_norm, w_out,
           g_mix_pre, g_mix_post, g_ffn_pre, g_ffn_post, ffn_w_up, ffn_conv, ffn_w_down):
    b, s, d = x.shape
    depth = w_in.shape[0]
    x2 = x.reshape(b * s, d)
    for l in range(depth):
        w_dn, w_sa, w_hg = _split_w_in(w_in[l])
        dn, sa, hg = _proj_call(x2, g_mix_pre[l].reshape(1, d), w_dn, w_sa, w_hg)
        o_a = _dn_call(dn.reshape(b, s, -1), dn_conv[l], dn_a_log[l], dn_dt_bias[l], dn_norm[l])
        o_b = _sa_call(sa.reshape(b, s, -1))
        o_c = _hg_call(hg.reshape(b, s, -1), hg_lb, hg_norm[l], l)
        x2 = _mix_ffn_call(
            x2, o_a.reshape(b * s, -1), o_b.reshape(b * s, -1), o_c.reshape(b * s, -1),
            w_out[l].astype(BF16),
            g_mix_post[l].reshape(1, d), g_ffn_pre[l].reshape(1, d),
            ffn_w_up[l].astype(BF16), ffn_conv[l], ffn_w_down[l].astype(BF16),
            g_ffn_post[l].reshape(1, d), s)
    return x2.reshape(b, s, d)
```

```python
import functools

import jax
import jax.numpy as jnp
from jax import lax
from jax.experimental import pallas as pl
from jax.experimental.pallas import tpu as pltpu

F32 = jnp.float32
BF16 = jnp.bfloat16
HIGHEST = lax.Precision.HIGHEST

LANES = 128
CHUNK = 64
CHUNK_SHIFT = 6
RMS_EPS = 1e-6
DN_DHEAD = 128
DN_HEADS = 4
DN_WIDTH = DN_HEADS * DN_DHEAD
DN_CONV = 4
SA_DHEAD = 64
SA_HEADS = 4
SA_WIDTH = SA_HEADS * SA_DHEAD
IDX_HEADS = 4
IDX_DHEAD = 64
IDX_TOPK_MAX = 256
Q_BLOCK = 256
SA_KEY_TILE = 512
SA_TILE_GROUP = 1
HG_DHEAD = 64
HG_HEADS = 4
HG_WIDTH = HG_HEADS * HG_DHEAD
HG_SUB = 16
FFN_CONV = 3
FF_TILE = 256
SEQ_TILE = 1024
DN_GROUP = 4
HG_GROUP = 4
INT_MIN = -2 ** 31

VMEM_LIMIT = 56 * 1024 * 1024

DN_DHEAD_SHIFT = 7
assert CHUNK == 1 << CHUNK_SHIFT and DN_DHEAD == 1 << DN_DHEAD_SHIFT


def _mm(a, b):
    return jnp.dot(a.astype(BF16), b.astype(BF16), preferred_element_type=F32)


def _mm_nt(a, b):
    return lax.dot_general(a.astype(BF16), b.astype(BF16), (((1,), (1,)), ((), ())),
                           preferred_element_type=F32)


def _mm_f32(a, b):
    return jnp.dot(a, b, precision=HIGHEST, preferred_element_type=F32)


def _sigmoid(x):
    return 1.0 / (1.0 + jnp.exp(-x))


def _silu(x):
    return x * _sigmoid(x)


def _softplus(x):
    return jnp.maximum(x, 0.0) + jnp.log1p(jnp.exp(-jnp.abs(x)))


def _rms(x, g):
    return x * lax.rsqrt(jnp.mean(x * x, axis=-1, keepdims=True) + RMS_EPS) * g


DN_COLS = 4 * DN_WIDTH + LANES
SA_COLS = 2 * SA_HEADS * LANES + 2 * LANES
HG_COLS = 4 * HG_WIDTH


def _proj_kernel(x_ref, g_ref, w_ref, od_ref, os_ref, oh_ref):
    h = _rms(x_ref[...], g_ref[...]).astype(BF16)
    y = jnp.dot(h, w_ref[...], preferred_element_type=F32)
    zeros = lambda n: jnp.zeros((y.shape[0], n), F32)
    pos = [0]

    def take(n):
        piece = y[:, pos[0]:pos[0] + n]
        pos[0] += n
        return piece

    def padded_heads(n_heads, width):
        return [p for _ in range(n_heads) for p in (take(width), zeros(LANES - width))]

    od_ref[...] = jnp.concatenate([take(4 * DN_WIDTH + 2 * DN_HEADS), zeros(LANES - 2 * DN_HEADS)],
                                  axis=1)
    b_q = padded_heads(SA_HEADS, SA_DHEAD)
    b_kv = [take(2 * SA_DHEAD)]
    b_qi = padded_heads(IDX_HEADS, IDX_DHEAD)
    b_kiw = [take(IDX_DHEAD + IDX_HEADS), zeros(LANES - IDX_DHEAD - IDX_HEADS)]
    os_ref[...] = jnp.concatenate(b_q + b_qi + b_kv + b_kiw, axis=1)
    oh_ref[...] = take(HG_COLS)


def _proj_call(x2, g, w, tm=512):
    m, d = x2.shape
    full = lambda a: pl.BlockSpec(a.shape, lambda i: (0, 0), pipeline_mode=pl.Buffered(1))
    rows = lambda n: pl.BlockSpec((tm, n), lambda i: (i, 0))
    widths = (DN_COLS, SA_COLS, HG_COLS)
    return pl.pallas_call(
        _proj_kernel,
        grid=(m // tm,),
        in_specs=[rows(d), full(g), full(w)],
        out_specs=[rows(n) for n in widths],
        out_shape=[jax.ShapeDtypeStruct((m, n), F32) for n in widths],
        compiler_params=pltpu.CompilerParams(dimension_semantics=("arbitrary",),
                                             vmem_limit_bytes=VMEM_LIMIT),
        name="proj",
    )(x2, g, w)


def _dn_kernel(q_ref, k_ref, v_ref, z_ref, gt_ref, cw_ref, par_ref, ng_ref, o_ref,
               state_ref, halo_ref, u0_ref, lhs_ref, qkd_ref, kdt_ref, gend_ref):
    C = CHUNK
    D = DN_DHEAD
    H = DN_HEADS
    R = H * C
    n_chunks = q_ref.shape[0] // C
    lane = lax.broadcasted_iota(jnp.int32, (C, LANES), 1)
    row8 = lax.broadcasted_iota(jnp.int32, (8, DN_WIDTH), 0)
    r64 = lax.broadcasted_iota(jnp.int32, (C, C), 0)
    c64 = lax.broadcasted_iota(jnp.int32, (C, C), 1)
    tril = (c64 <= r64).astype(F32)
    rr = lax.broadcasted_iota(jnp.int32, (R, R), 0)
    cc = lax.broadcasted_iota(jnp.int32, (R, R), 1)
    same = (rr >> CHUNK_SHIFT) == (cc >> CHUNK_SHIFT)
    causal = same & (cc <= rr)
    strict = same & (cc < rr)
    eyef = (cc == rr).astype(F32)
    own_state = (lax.broadcasted_iota(jnp.int32, (R, H * D), 0) >> CHUNK_SHIFT) == \
        (lax.broadcasted_iota(jnp.int32, (R, H * D), 1) >> DN_DHEAD_SHIFT)
    neg_a = -jnp.exp(par_ref[0:1, :])
    dt_bias = par_ref[1:2, :]
    ng = ng_ref[...]
    srcs = (q_ref, k_ref, v_ref)

    @pl.when(pl.program_id(1) == 0)
    def _():
        state_ref[...] = jnp.zeros_like(state_ref)
        halo_ref[...] = jnp.zeros_like(halo_ref)

    def conv_silu(i, r0, n):
        ref = srcs[i]
        w = cw_ref[:, i * DN_WIDTH:(i + 1) * DN_WIDTH]
        cur = ref[pl.ds(r0, C), :]
        rp = pl.multiple_of(jnp.maximum(r0 - 8, 0), 8)
        prev = jnp.where(n > 0, ref[pl.ds(rp, 8), :], halo_ref[i])
        acc = cur * w[DN_CONV - 1:DN_CONV, :]
        for d in range(1, DN_CONV):
            rolled = pltpu.roll(cur, d, 0)
            head = jnp.where(row8 < d, pltpu.roll(prev, d, 0), rolled[:8, :])
            sh = jnp.concatenate([head, rolled[8:, :]], axis=0)
            acc = acc + sh * w[DN_CONV - 1 - d:DN_CONV - d, :]
        return _silu(acc)

    def l2norm(x):
        return x * lax.rsqrt(jnp.sum(x * x, axis=-1, keepdims=True) + 1e-6)

    def stack(a):
        return jnp.concatenate([a[:, h * D:(h + 1) * D] for h in range(H)], axis=0)

    def column(a, first_lane):
        return jnp.concatenate(
            [jnp.sum(jnp.where(lane == first_lane + h, a, 0.0), axis=1, keepdims=True)
             for h in range(H)], axis=0)

    def setup(n):
        r0 = pl.multiple_of(n * C, C)
        q = l2norm(stack(conv_silu(0, r0, n))) * (D ** -0.5)
        k = l2norm(stack(conv_silu(1, r0, n)))
        v = stack(conv_silu(2, r0, n))
        gt = gt_ref[pl.ds(r0, C), :]
        beta = column(_sigmoid(gt), 0)
        g_cum = _mm_f32(tril, neg_a * _softplus(gt + dt_bias))
        gb = jnp.broadcast_to(column(g_cum, H), (R, LANES))
        gcol = jnp.concatenate([gb, gb], axis=1)
        decay = jnp.exp(jnp.where(causal, gcol - gcol.T, -jnp.inf))

        qkk = _mm_nt(jnp.concatenate([q, k], axis=0), k)
        x = jnp.where(strict, -(beta * qkk[R:, :] * decay), 0.0)
        return x, (q, k, v, beta, gb, qkk, decay)

    def finish(n, t, aux):
        q, k, v, beta, gb, qkk, decay = aux
        eg = jnp.exp(gb)
        uw = _mm(t, jnp.concatenate([beta * v, (beta * eg) * k], axis=1))
        g_end = jnp.concatenate(
            [jnp.broadcast_to(gb[h * C + C - 1:(h + 1) * C, :], (C, LANES)) for h in range(H)],
            axis=0)
        qeg = q * eg
        u0_ref[n] = uw[:, :D]
        for h in range(H):
            rows = slice(h * C, (h + 1) * C)
            lhs_ref[n, h] = jnp.concatenate([uw[rows, D:], qeg[rows, :]], axis=0).astype(BF16)
        qkd_ref[n] = (qkk[:R, :] * decay).astype(BF16)
        kdt_ref[n] = (k * jnp.exp(g_end - gb)).T.astype(BF16)
        gend_ref[n] = jnp.concatenate(
            [jnp.broadcast_to(jnp.exp(gb[h * C + C - 1:(h + 1) * C, :]), (8, LANES))
             for h in range(H)], axis=1)

    def prepare(i, carry):
        chunks = [i * DN_GROUP + j for j in range(DN_GROUP)]
        xs, auxs = zip(*[setup(n) for n in chunks])
        ts = [eyef + x for x in xs]
        ps = list(xs)
        for _ in range(5):
            ps = [_mm(p, p) for p in ps]
            ts = [t + _mm(t, p) for t, p in zip(ts, ps)]
        for n, t, aux in zip(chunks, ts, auxs):
            finish(n, t, aux)
        return carry

    lax.fori_loop(0, n_chunks // DN_GROUP, prepare, 0)

    def advance(n, carry):
        r0 = pl.multiple_of(n * C, C)
        state = state_ref[...]
        ws = [jnp.dot(lhs_ref[n, h], state[:, h * D:(h + 1) * D].astype(BF16),
                      preferred_element_type=F32) for h in range(H)]
        u = u0_ref[n] - jnp.concatenate([w[:C, :] for w in ws], axis=0)
        o = jnp.concatenate([w[C:, :] for w in ws], axis=0) \
            + jnp.dot(qkd_ref[n], u.astype(BF16), preferred_element_type=F32)
        u_wide = jnp.where(own_state, jnp.concatenate([u] * H, axis=1), 0.0).astype(BF16)
        state_ref[...] = state * gend_ref[n][0:1, :] \
            + jnp.dot(kdt_ref[n], u_wide, preferred_element_type=F32)
        o = _rms(o, ng) * _silu(stack(z_ref[pl.ds(r0, C), :]))
        o_ref[pl.ds(r0, C), :] = jnp.concatenate(
            [o[h * C:(h + 1) * C, :] for h in range(H)], axis=1)
        return carry

    lax.fori_loop(0, n_chunks, advance, 0, unroll=2)
    last = q_ref.shape[0] - 8
    for i in range(3):
        halo_ref[i] = srcs[i][last:, :]


def _dn_call(dn, conv_w, a_log, dt_bias, norm_g):
    b, s, _ = dn.shape
    st = min(SEQ_TILE, s)
    nc = st // CHUNK
    rows = DN_HEADS * CHUNK
    par = jnp.zeros((8, LANES), F32)
    par = par.at[0, DN_HEADS:2 * DN_HEADS].set(a_log).at[1, DN_HEADS:2 * DN_HEADS].set(dt_bias)
    col = lambda blk: pl.BlockSpec((None, st, DN_WIDTH), lambda i, j, blk=blk: (i, j, blk))
    full = lambda a: pl.BlockSpec(a.shape, lambda i, j: (0, 0))
    ng = norm_g.reshape(1, LANES)
    return pl.pallas_call(
        _dn_kernel,
        grid=(b, s // st),
        in_specs=[col(0), col(1), col(2), col(3),
                  pl.BlockSpec((None, st, LANES), lambda i, j: (i, j, 4 * DN_HEADS)),
                  full(conv_w), full(par), full(ng)],
        out_specs=pl.BlockSpec((None, st, DN_WIDTH), lambda i, j: (i, j, 0)),
        out_shape=jax.ShapeDtypeStruct((b, s, DN_WIDTH), F32),
        scratch_shapes=[pltpu.VMEM((DN_DHEAD, DN_WIDTH), F32),
                        pltpu.VMEM((3, 8, DN_WIDTH), F32),
                        pltpu.VMEM((nc, rows, DN_DHEAD), F32),
                        pltpu.VMEM((nc, DN_HEADS, 2 * CHUNK, DN_DHEAD), BF16),
                        pltpu.VMEM((nc, rows, rows), BF16),
                        pltpu.VMEM((nc, DN_DHEAD, rows), BF16),
                        pltpu.VMEM((nc, 8, DN_WIDTH), F32)],
        compiler_params=pltpu.CompilerParams(dimension_semantics=("arbitrary", "arbitrary"),
                                             vmem_limit_bytes=VMEM_LIMIT),
        name="deltanet",
    )(dn, dn, dn, dn, dn, conv_w, par, ng)


def _hg_kernel(q_ref, f_ref, i_ref, g_ref, lb_ref, ng_ref, o_ref, state_ref, intra_ref,
               qdec_ref, kdec_ref, vt_ref, gend_ref, *, layer):
    C = CHUNK
    n_chunks = q_ref.shape[0] // C
    npair = HG_WIDTH // LANES
    nsub = C // HG_SUB
    lane = lax.broadcasted_iota(jnp.int32, (C, LANES), 1)
    row = lax.broadcasted_iota(jnp.int32, (C, LANES), 0)
    r64 = lax.broadcasted_iota(jnp.int32, (C, C), 0)
    c64 = lax.broadcasted_iota(jnp.int32, (C, C), 1)
    tril = (c64 <= r64).astype(F32)
    rl = lax.broadcasted_iota(jnp.int32, (LANES, LANES), 0)
    cl = lax.broadcasted_iota(jnp.int32, (LANES, LANES), 1)
    same_head = (rl < HG_DHEAD) == (cl < HG_DHEAD)
    head_ones = same_head.astype(BF16)
    head_mean = same_head.astype(F32) * (1.0 / HG_DHEAD)
    sub_row = lax.broadcasted_iota(jnp.int32, (HG_SUB, LANES), 0)
    pairs = []
    size = HG_SUB
    while size < C:
        pairs += [(first, size) for first in range(size, C, 2 * size)]
        size *= 2
    head0 = (lax.broadcasted_iota(jnp.int32, (C, LANES * len(pairs)), 1) & (LANES - 1)) < HG_DHEAD

    @pl.when(pl.program_id(1) == 0)
    def _():
        state_ref[...] = jnp.zeros_like(state_ref)

    lbr = lb_ref[...]
    e = jnp.exp(lbr - jnp.max(lbr, axis=0, keepdims=True))
    sm = e / jnp.sum(e, axis=0, keepdims=True)
    lb_all = jnp.zeros((1, HG_WIDTH), F32)
    for i in range(1, layer + 1):
        lb_all = lb_all + sm[i:i + 1, :]

    def load(unit):
        n, pr = unit
        r0 = pl.multiple_of(n * C, C)
        sl = slice(pr * LANES, (pr + 1) * LANES)
        lb = lb_all[:, sl]
        fg = lb + (1.0 - lb) * _sigmoid(f_ref[pl.ds(r0, C), sl])
        qc = _silu(q_ref[pl.ds(r0, C), sl])
        vc = i_ref[pl.ds(r0, C), sl]
        bc = _mm_f32(tril, jnp.log(fg))
        return qc, 1.0 - fg, vc, bc

    def diag_products(vals):
        qc, kc, vc, bc = vals
        parts = []
        for j in range(nsub):
            lo = j * HG_SUB
            bblk = bc[lo:lo + HG_SUB, :]
            qblk = qc[lo:lo + HG_SUB, :]
            for s in range(HG_SUB):
                dec = jnp.exp(jnp.where(sub_row >= s, bblk - bc[lo + s:lo + s + 1, :], -jnp.inf))
                parts.append((qblk * kc[lo + s:lo + s + 1, :] * dec).astype(BF16))
        return jnp.dot(jnp.concatenate(parts, axis=0), head_ones, preferred_element_type=F32)

    def off_diag_scores(vals):
        qc, kc, vc, bc = vals
        qts, kts = [], []
        for first, size in pairs:
            bref = bc[first:first + 1, :]
            qmask = (row >= first) & (row < first + size)
            kmask = (row >= first - size) & (row < first)
            qts.append(jnp.where(qmask, qc * jnp.exp(jnp.where(qmask, bc - bref, 0.0)), 0.0))
            kts.append(jnp.where(kmask, kc * jnp.exp(jnp.where(kmask, bref - bc, 0.0)), 0.0))
        qt = jnp.concatenate(qts, axis=1)
        kt = jnp.concatenate(kts, axis=1).astype(BF16)
        return _mm_nt(jnp.where(head0, qt, 0.0), kt), _mm_nt(jnp.where(head0, 0.0, qt), kt)

    def store(unit, vals, rs, a01):
        n, pr = unit
        qc, kc, vc, bc = vals
        diag = []
        for j in range(nsub):
            lo = j * HG_SUB
            acc = jnp.zeros((HG_SUB, LANES), F32)
            for s in range(HG_SUB):
                base = (j * HG_SUB + s) * HG_SUB
                acc = acc + rs[base:base + HG_SUB, :] * vc[lo + s:lo + s + 1, :]
            diag.append(acc)
        b_end = bc[C - 1:C, :]
        intra_ref[n, pr] = jnp.concatenate(diag, axis=0) \
            + jnp.where(lane < HG_DHEAD, _mm(a01[0], vc), _mm(a01[1], vc))
        qdec_ref[n, pr] = (qc * jnp.exp(bc)).astype(BF16)
        kdec_ref[n, pr] = (kc * jnp.exp(b_end - bc)).astype(BF16)
        vt_ref[n, pr] = vc.T.astype(BF16)
        gend_ref[n, pr] = jnp.broadcast_to(jnp.exp(b_end), (8, LANES))

    def prepare(i, carry):
        units = [(i * HG_GROUP + c, pr) for c in range(HG_GROUP) for pr in range(npair)]
        vals = [load(u) for u in units]
        rss = [diag_products(v) for v in vals]
        a01s = [off_diag_scores(v) for v in vals]
        for u, v, rs, a01 in zip(units, vals, rss, a01s):
            store(u, v, rs, a01)
        return carry

    lax.fori_loop(0, n_chunks // HG_GROUP, prepare, 0)

    def advance(i, carry):
        chunks = [i * HG_GROUP + c for c in range(HG_GROUP)]
        states = []
        for pr in range(npair):
            seq = [state_ref[pr]]
            for n in chunks:
                seq.append(jnp.where(
                    same_head,
                    seq[-1] * gend_ref[n, pr][0:1, :]
                    + jnp.dot(vt_ref[n, pr], kdec_ref[n, pr], preferred_element_type=F32), 0.0))
            state_ref[pr] = seq[-1]
            states.append(seq)
        os = [[intra_ref[n, pr] + lax.dot_general(
            qdec_ref[n, pr], states[pr][c].astype(BF16), (((1,), (1,)), ((), ())),
            preferred_element_type=F32) for pr in range(npair)] for c, n in enumerate(chunks)]
        mss = [[_mm_f32(o * o, head_mean) for o in row_os] for row_os in os]
        for c, n in enumerate(chunks):
            r0 = pl.multiple_of(n * C, C)
            o_ref[pl.ds(r0, C), :] = jnp.concatenate(
                [os[c][pr] * lax.rsqrt(mss[c][pr] + RMS_EPS) * ng_ref[...]
                 * _silu(g_ref[pl.ds(r0, C), pr * LANES:(pr + 1) * LANES])
                 for pr in range(npair)], axis=1)
        return carry

    lax.fori_loop(0, n_chunks // HG_GROUP, advance, 0)


def _hg_call(hg, hg_lb, norm_g, layer):
    b, s, _ = hg.shape
    st = min(SEQ_TILE, s)
    nc = st // CHUNK
    npair = HG_WIDTH // LANES
    col = lambda blk: pl.BlockSpec((None, st, HG_WIDTH), lambda i, j, blk=blk: (i, j, blk))
    full = lambda a: pl.BlockSpec(a.shape, lambda i, j: (0, 0))
    ng2 = jnp.tile(norm_g, LANES // HG_DHEAD).reshape(1, LANES)
    return pl.pallas_call(
        functools.partial(_hg_kernel, layer=layer),
        grid=(b, s // st),
        in_specs=[col(0), col(1), col(2), col(3), full(hg_lb), full(ng2)],
        out_specs=pl.BlockSpec((None, st, HG_WIDTH), lambda i, j: (i, j, 0)),
        out_shape=jax.ShapeDtypeStruct((b, s, HG_WIDTH), F32),
        scratch_shapes=[pltpu.VMEM((npair, LANES, LANES), F32),
                        pltpu.VMEM((nc, npair, CHUNK, LANES), F32),
                        pltpu.VMEM((nc, npair, CHUNK, LANES), BF16),
                        pltpu.VMEM((nc, npair, CHUNK, LANES), BF16),
                        pltpu.VMEM((nc, npair, LANES, CHUNK), BF16),
                        pltpu.VMEM((nc, npair, 8, LANES), F32)],
        compiler_params=pltpu.CompilerParams(dimension_semantics=("arbitrary", "arbitrary"),
                                             vmem_limit_bytes=VMEM_LIMIT),
        name="hgrn2",
    )(hg, hg, hg, hg, hg_lb, ng2)


def _sa_kernel(q_ref, qi_ref, wq_ref, kv_ref, kiw_ref, o_ref, kvb_ref, kvt_ref, kib_ref,
               key_ref, bias_ref, acc_ref, tri_ref, *, topk):
    T = Q_BLOCK
    TK = SA_KEY_TILE
    NACC = 4
    s_len = kv_ref.shape[0]
    j = pl.program_id(1)
    nkt = (j * T + T - 1) // TK + 1
    row_k = lax.broadcasted_iota(jnp.int32, (TK, T), 0)
    qpos = j * T + lax.broadcasted_iota(jnp.int32, (1, T), 1)
    first_hidden = ((qpos >> CHUNK_SHIFT) + 1) << CHUNK_SHIFT

    @pl.when(j == 0)
    def _():
        def prep(t, c):
            r0 = pl.multiple_of(t * TK, TK)
            kv = kv_ref[pl.ds(r0, TK), :]
            kvb_ref[pl.ds(r0, TK), :] = kv.astype(BF16)
            kvt_ref[:, pl.ds(r0, TK)] = kv.T.astype(BF16)
            kib_ref[pl.ds(r0, TK), :] = kiw_ref[pl.ds(r0, TK), :].astype(BF16)
            return c
        lax.fori_loop(0, s_len // TK, prep, 0)
        tri_ref[...] = jnp.where(lax.broadcasted_iota(jnp.int32, (TK, TK), 1)
                                 <= lax.broadcasted_iota(jnp.int32, (TK, TK), 0),
                                 1.0, 0.0).astype(BF16)

    w_t = wq_ref[...].T * (IDX_HEADS ** -0.5 * IDX_DHEAD ** -0.5)
    w_all = jnp.concatenate(
        [w_t[IDX_DHEAD + hh:IDX_DHEAD + hh + 1, :] for hh in range(IDX_HEADS)], axis=1)

    def stack_heads(ref):
        return jnp.concatenate(
            [ref[:, hh * LANES:(hh + 1) * LANES] for hh in range(ref.shape[1] // LANES)], axis=0)
    qi_all = stack_heads(qi_ref).astype(BF16)

    G = SA_TILE_GROUP
    n_pairs = (nkt + G - 1) // G

    def score_pair(i, c):
        kts = [G * i + t for t in range(G)]
        r0s = [pl.multiple_of(kt * TK, TK) for kt in kts]
        ds = [_mm_nt(kib_ref[pl.ds(r0, TK), :], qi_all) for r0 in r0s]
        for kt, r0, d in zip(kts, r0s, ds):
            d = jnp.maximum(d, 0.0) * w_all
            sc = d[:, :T]
            for hh in range(1, IDX_HEADS):
                sc = sc + d[:, hh * T:(hh + 1) * T]
            sc = jnp.where(sc == 0.0, 0.0, sc)
            bits = lax.bitcast_convert_type(sc, jnp.int32)
            key = jnp.where(bits < 0, bits ^ 0x7FFFFFFF, bits)
            key_ref[pl.ds(r0, TK), :] = jnp.where(kt * TK + row_k >= first_hidden, INT_MIN, key)
        return c
    lax.fori_loop(0, n_pairs, score_pair, 0)

    def count(pred):
        def f(kt, acc):
            r0 = pl.multiple_of(kt * TK, TK)
            m = pred(key_ref[pl.ds(r0, TK), :], kt * TK + row_k).astype(jnp.int32)
            return acc + jnp.sum(m.reshape(TK // (8 * NACC), NACC, 8, T), axis=0)
        acc = lax.fori_loop(0, nkt, f, jnp.zeros((NACC, 8, T), jnp.int32))
        return jnp.sum(jnp.sum(acc, axis=0), axis=0, keepdims=True)

    c0 = count(lambda k, i: k >= 0)
    nonneg = c0 >= topk
    thr0 = jnp.where(nonneg, 0, INT_MIN).astype(jnp.int32)
    n_gt0 = jnp.where(nonneg, 0, c0)

    def bit_step(i, carry):
        thr, n_ge, n_gt = carry
        cand = thr + jnp.left_shift(jnp.int32(1), 30 - i)
        c = count(lambda k, idx: k >= cand)
        take = c >= topk
        return jnp.where(take, cand, thr), jnp.where(take, c, n_ge), jnp.where(take, n_gt, c)
    thr, n_ge, n_gt = lax.fori_loop(0, 31, bit_step, (thr0, c0, n_gt0))

    has_ties = jnp.max(((n_ge > topk) & (thr > INT_MIN)).astype(jnp.int32)) > 0

    def bias_with_ties():
        need = jnp.where(thr > INT_MIN, topk - n_gt, 0).astype(F32)

        def bias_tile(kt, seen):
            r0 = pl.multiple_of(kt * TK, TK)
            k = key_ref[pl.ds(r0, TK), :]
            tie = k == thr
            rank = seen + jnp.dot(tri_ref[...], jnp.where(tie, 1.0, 0.0).astype(BF16),
                                  preferred_element_type=F32)
            take = jnp.where(tie, rank, jnp.inf) <= need
            bias_ref[pl.ds(r0, TK), :] = jnp.where(k > thr, 0.0, jnp.where(take, 0.0, -jnp.inf))
            return rank[TK - 1:TK, :]
        lax.fori_loop(0, G * n_pairs, bias_tile, jnp.zeros((1, T), F32))

    def bias_without_ties():
        thr_lo = jnp.maximum(thr, INT_MIN + 1)

        def bias_tile(kt, c):
            r0 = pl.multiple_of(kt * TK, TK)
            bias_ref[pl.ds(r0, TK), :] = jnp.where(key_ref[pl.ds(r0, TK), :] >= thr_lo,
                                                   0.0, -jnp.inf)
            return c
        lax.fori_loop(0, G * n_pairs, bias_tile, 0)
    lax.cond(has_ties, bias_with_ties, bias_without_ties)

    acc_ref[...] = jnp.zeros_like(acc_ref)
    q_all = (stack_heads(q_ref) * (SA_DHEAD ** -0.5)).astype(BF16)

    def att_pair(i, carry):
        m_old, den = carry
        r0s = [pl.multiple_of((G * i + t) * TK, TK) for t in range(G)]
        lgs = [_mm_nt(kvb_ref[pl.ds(r0, TK), :], q_all) for r0 in r0s]
        lgs = [lg + jnp.concatenate([bias_ref[pl.ds(r0, TK), :]] * SA_HEADS, axis=1)
               for lg, r0 in zip(lgs, r0s)]
        m_new = jnp.maximum(m_old, jnp.max(functools.reduce(jnp.maximum, lgs), axis=0,
                                           keepdims=True))
        m_use = jnp.where(m_new == -jnp.inf, 0.0, m_new)
        alpha = jnp.exp(m_old - m_use)
        ps = [jnp.exp(lg - m_use) for lg in lgs]
        pvs = [jnp.dot(kvt_ref[:, pl.ds(r0, TK)], p.astype(BF16), preferred_element_type=F32)
               for p, r0 in zip(ps, r0s)]
        acc_ref[...] = acc_ref[...] * alpha + sum(pvs[1:], pvs[0])
        return m_new, den * alpha + jnp.sum(sum(ps[1:], ps[0]), axis=0, keepdims=True)
    hq = SA_HEADS * T
    _, den = lax.fori_loop(0, n_pairs, att_pair,
                           (jnp.full((1, hq), -jnp.inf, F32), jnp.zeros((1, hq), F32)))
    res = acc_ref[...] / den
    for hh in range(SA_HEADS):
        o_ref[:, hh * SA_DHEAD:(hh + 1) * SA_DHEAD] = res[:, hh * T:(hh + 1) * T].T[:, SA_DHEAD:]


def _sa_call(sa):
    b, s, _ = sa.shape
    topk = min(IDX_TOPK_MAX, s // 4)
    T = Q_BLOCK
    assert s % (SA_TILE_GROUP * SA_KEY_TILE) == 0
    qw = SA_HEADS * LANES
    seq = lambda blk: pl.BlockSpec((None, s, LANES), lambda i, j, blk=blk: (i, 0, blk))
    return pl.pallas_call(
        functools.partial(_sa_kernel, topk=topk),
        grid=(b, s // T),
        in_specs=[pl.BlockSpec((None, T, qw), lambda i, j: (i, j, 0)),
                  pl.BlockSpec((None, T, qw), lambda i, j: (i, j, 1)),
                  pl.BlockSpec((None, T, LANES), lambda i, j: (i, j, 2 * SA_HEADS + 1)),
                  seq(2 * SA_HEADS), seq(2 * SA_HEADS + 1)],
        out_specs=pl.BlockSpec((None, T, SA_WIDTH), lambda i, j: (i, j, 0)),
        out_shape=jax.ShapeDtypeStruct((b, s, SA_WIDTH), F32),
        scratch_shapes=[pltpu.VMEM((s, LANES), BF16), pltpu.VMEM((LANES, s), BF16),
                        pltpu.VMEM((s, LANES), BF16), pltpu.VMEM((s, T), jnp.int32),
                        pltpu.VMEM((s, T), F32), pltpu.VMEM((LANES, SA_HEADS * T), F32),
                        pltpu.VMEM((SA_KEY_TILE, SA_KEY_TILE), BF16)],
        compiler_params=pltpu.CompilerParams(dimension_semantics=("arbitrary", "arbitrary"),
                                             vmem_limit_bytes=VMEM_LIMIT),
        name="dsa",
    )(sa, sa, sa, sa, sa)


def _mix_ffn_kernel(x_ref, oa_ref, ob_ref, oc_ref, wo_ref, gpost_ref, gpre_ref,
                    wup_ref, cw_ref, wdn_ref, gout_ref, o_ref, carry_ref, act_ref, *,
                    tiles_per_seq):
    tm = x_ref.shape[0]
    d_ff = wdn_ref.shape[0]
    first = pl.program_id(0) % tiles_per_seq == 0
    mixed = jnp.concatenate([oa_ref[...], ob_ref[...], oc_ref[...]], axis=1).astype(BF16)
    mix = jnp.dot(mixed, wo_ref[...], preferred_element_type=F32)
    x1 = x_ref[...] + _rms(mix, gpost_ref[...])
    h = _rms(x1, gpre_ref[...]).astype(BF16)
    row = lax.broadcasted_iota(jnp.int32, (tm, FF_TILE), 0)

    def conv(c0):
        u = jnp.dot(h, wup_ref[:, c0:c0 + FF_TILE], preferred_element_type=F32)
        prev = jnp.where(first, 0.0, carry_ref[:, c0:c0 + FF_TILE])
        carry_ref[:, c0:c0 + FF_TILE] = u[tm - 8:, :]
        u1 = jnp.where(row == 0, prev[7:8, :], pltpu.roll(u, 1, 0))
        u2 = jnp.where(row == 0, prev[6:7, :],
                       jnp.where(row == 1, prev[7:8, :], pltpu.roll(u, 2, 0)))
        w = cw_ref[:, c0:c0 + FF_TILE]
        return u2 * w[0:1, :] + u1 * w[1:2, :] + u * w[2:3, :]

    for c in range(d_ff // FF_TILE):
        gate = conv(c * FF_TILE)
        val = conv(d_ff + c * FF_TILE)
        inner = gate + 0.044715 * (gate * gate * gate)
        act = 0.5 * gate * (1.0 + jnp.tanh(0.7978845608028654 * inner)) * val
        act_ref[:, c * FF_TILE:(c + 1) * FF_TILE] = act.astype(BF16)
    down = jnp.dot(act_ref[...], wdn_ref[...], preferred_element_type=F32)
    o_ref[...] = x1 + _rms(down, gout_ref[...])


def _mix_ffn_call(x2, oa, ob, oc, wo, g_post, g_pre, w_up, conv_w, w_down, g_out, seq,
                  tm=512):
    m, d = x2.shape
    rows = lambda a: pl.BlockSpec((tm, a.shape[1]), lambda i: (i, 0))
    full = lambda a: pl.BlockSpec(a.shape, lambda i: (0, 0), pipeline_mode=pl.Buffered(1))
    return pl.pallas_call(
        functools.partial(_mix_ffn_kernel, tiles_per_seq=seq // tm),
        grid=(m // tm,),
        in_specs=[rows(x2), rows(oa), rows(ob), rows(oc), full(wo),
                  full(g_post), full(g_pre), full(w_up), full(conv_w), full(w_down), full(g_out)],
        out_specs=pl.BlockSpec((tm, d), lambda i: (i, 0)),
        out_shape=jax.ShapeDtypeStruct((m, d), F32),
        scratch_shapes=[pltpu.VMEM((8, w_up.shape[1]), F32),
                        pltpu.VMEM((tm, w_down.shape[0]), BF16)],
        compiler_params=pltpu.CompilerParams(dimension_semantics=("arbitrary",),
                                             vmem_limit_bytes=VMEM_LIMIT),
        name="mix_ffn",
    )(x2, oa, ob, oc, wo, g_post, g_pre, w_up, conv_w, w_down, g_out)


def kernel(x, w_in, dn_conv, dn_a_log, dn_dt_bias, dn_norm, hg_lb, hg_norm, w_out,
           g_mix_pre, g_mix_post, g_ffn_pre, g_ffn_post, ffn_w_up, ffn_conv, ffn_w_down):
    b, s, d = x.shape
    depth = w_in.shape[0]
    x2 = x.reshape(b * s, d)
    for l in range(depth):
        dn, sa, hg = _proj_call(x2, g_mix_pre[l].reshape(1, d), w_in[l].astype(BF16))
        o_a = _dn_call(dn.reshape(b, s, -1), dn_conv[l], dn_a_log[l], dn_dt_bias[l], dn_norm[l])
        o_b = _sa_call(sa.reshape(b, s, -1))
        o_c = _hg_call(hg.reshape(b, s, -1), hg_lb, hg_norm[l], l)
        x2 = _mix_ffn_call(
            x2, o_a.reshape(b * s, -1), o_b.reshape(b * s, -1), o_c.reshape(b * s, -1),
            w_out[l].astype(BF16),
            g_mix_post[l].reshape(1, d), g_ffn_pre[l].reshape(1, d),
            ffn_w_up[l].astype(BF16), ffn_conv[l], ffn_w_down[l].astype(BF16),
            g_ffn_post[l].reshape(1, d), s)
    return x2.reshape(b, s, d)
```

```python
import functools

import jax
import jax.numpy as jnp
from jax import lax
from jax.experimental import pallas as pl
from jax.experimental.pallas import tpu as pltpu

F32 = jnp.float32
BF16 = jnp.bfloat16
HIGHEST = lax.Precision.HIGHEST

LANES = 128
CHUNK = 64
CHUNK_SHIFT = 6
RMS_EPS = 1e-6
DN_DHEAD = 128
DN_HEADS = 4
DN_WIDTH = DN_HEADS * DN_DHEAD
DN_CONV = 4
SA_DHEAD = 64
SA_HEADS = 4
SA_WIDTH = SA_HEADS * SA_DHEAD
IDX_HEADS = 4
IDX_DHEAD = 64
IDX_TOPK_MAX = 256
Q_BLOCK = 256
SA_KEY_TILE = 512
HG_DHEAD = 64
HG_HEADS = 4
HG_WIDTH = HG_HEADS * HG_DHEAD
HG_SUB = 16
FFN_CONV = 3
FF_TILE = 256
SEQ_TILE = 1024
DN_GROUP = 4
HG_GROUP = 4
INT_MIN = -2 ** 31

VMEM_LIMIT = 56 * 1024 * 1024

DN_DHEAD_SHIFT = 7
assert CHUNK == 1 << CHUNK_SHIFT and DN_DHEAD == 1 << DN_DHEAD_SHIFT


def _mm(a, b):
    return jnp.dot(a.astype(BF16), b.astype(BF16), preferred_element_type=F32)


def _mm_nt(a, b):
    return lax.dot_general(a.astype(BF16), b.astype(BF16), (((1,), (1,)), ((), ())),
                           preferred_element_type=F32)


def _mm_f32(a, b):
    return jnp.dot(a, b, precision=HIGHEST, preferred_element_type=F32)


def _sigmoid(x):
    return 1.0 / (1.0 + jnp.exp(-x))


def _silu(x):
    return x * _sigmoid(x)


def _softplus(x):
    return jnp.maximum(x, 0.0) + jnp.log1p(jnp.exp(-jnp.abs(x)))


def _rms(x, g):
    return x * lax.rsqrt(jnp.mean(x * x, axis=-1, keepdims=True) + RMS_EPS) * g


DN_COLS = 4 * DN_WIDTH + LANES
SA_COLS = 2 * SA_HEADS * LANES + 2 * LANES
HG_COLS = 4 * HG_WIDTH


def _proj_kernel(x_ref, g_ref, w_ref, od_ref, os_ref, oh_ref):
    h = _rms(x_ref[...], g_ref[...]).astype(BF16)
    y = jnp.dot(h, w_ref[...], preferred_element_type=F32)
    zeros = lambda n: jnp.zeros((y.shape[0], n), F32)
    pos = [0]

    def take(n):
        piece = y[:, pos[0]:pos[0] + n]
        pos[0] += n
        return piece

    def padded_heads(n_heads, width):
        return [p for _ in range(n_heads) for p in (take(width), zeros(LANES - width))]

    od_ref[...] = jnp.concatenate([take(4 * DN_WIDTH + 2 * DN_HEADS), zeros(LANES - 2 * DN_HEADS)],
                                  axis=1)
    b_q = padded_heads(SA_HEADS, SA_DHEAD)
    b_kv = [take(2 * SA_DHEAD)]
    b_qi = padded_heads(IDX_HEADS, IDX_DHEAD)
    b_kiw = [take(IDX_DHEAD + IDX_HEADS), zeros(LANES - IDX_DHEAD - IDX_HEADS)]
    os_ref[...] = jnp.concatenate(b_q + b_qi + b_kv + b_kiw, axis=1)
    oh_ref[...] = take(HG_COLS)


def _proj_call(x2, g, w, tm=512):
    m, d = x2.shape
    full = lambda a: pl.BlockSpec(a.shape, lambda i: (0, 0), pipeline_mode=pl.Buffered(1))
    rows = lambda n: pl.BlockSpec((tm, n), lambda i: (i, 0))
    widths = (DN_COLS, SA_COLS, HG_COLS)
    return pl.pallas_call(
        _proj_kernel,
        grid=(m // tm,),
        in_specs=[rows(d), full(g), full(w)],
        out_specs=[rows(n) for n in widths],
        out_shape=[jax.ShapeDtypeStruct((m, n), F32) for n in widths],
        compiler_params=pltpu.CompilerParams(dimension_semantics=("arbitrary",),
                                             vmem_limit_bytes=VMEM_LIMIT),
        name="proj",
    )(x2, g, w)


def _dn_kernel(q_ref, k_ref, v_ref, z_ref, gt_ref, cw_ref, par_ref, ng_ref, o_ref,
               state_ref, halo_ref, u0_ref, lhs_ref, qkd_ref, kdt_ref, gend_ref):
    C = CHUNK
    D = DN_DHEAD
    H = DN_HEADS
    R = H * C
    n_chunks = q_ref.shape[0] // C
    lane = lax.broadcasted_iota(jnp.int32, (C, LANES), 1)
    row8 = lax.broadcasted_iota(jnp.int32, (8, DN_WIDTH), 0)
    r64 = lax.broadcasted_iota(jnp.int32, (C, C), 0)
    c64 = lax.broadcasted_iota(jnp.int32, (C, C), 1)
    tril = (c64 <= r64).astype(F32)
    rr = lax.broadcasted_iota(jnp.int32, (R, R), 0)
    cc = lax.broadcasted_iota(jnp.int32, (R, R), 1)
    same = (rr >> CHUNK_SHIFT) == (cc >> CHUNK_SHIFT)
    causal = same & (cc <= rr)
    strict = same & (cc < rr)
    eyef = (cc == rr).astype(F32)
    own_state = (lax.broadcasted_iota(jnp.int32, (R, H * D), 0) >> CHUNK_SHIFT) == \
        (lax.broadcasted_iota(jnp.int32, (R, H * D), 1) >> DN_DHEAD_SHIFT)
    neg_a = -jnp.exp(par_ref[0:1, :])
    dt_bias = par_ref[1:2, :]
    ng = ng_ref[...]
    srcs = (q_ref, k_ref, v_ref)

    @pl.when(pl.program_id(1) == 0)
    def _():
        state_ref[...] = jnp.zeros_like(state_ref)
        halo_ref[...] = jnp.zeros_like(halo_ref)

    def conv_silu(i, r0, n):
        ref = srcs[i]
        w = cw_ref[:, i * DN_WIDTH:(i + 1) * DN_WIDTH]
        cur = ref[pl.ds(r0, C), :]
        rp = pl.multiple_of(jnp.maximum(r0 - 8, 0), 8)
        prev = jnp.where(n > 0, ref[pl.ds(rp, 8), :], halo_ref[i])
        acc = cur * w[DN_CONV - 1:DN_CONV, :]
        for d in range(1, DN_CONV):
            rolled = pltpu.roll(cur, d, 0)
            head = jnp.where(row8 < d, pltpu.roll(prev, d, 0), rolled[:8, :])
            sh = jnp.concatenate([head, rolled[8:, :]], axis=0)
            acc = acc + sh * w[DN_CONV - 1 - d:DN_CONV - d, :]
        return _silu(acc)

    def l2norm(x):
        return x * lax.rsqrt(jnp.sum(x * x, axis=-1, keepdims=True) + 1e-6)

    def stack(a):
        return jnp.concatenate([a[:, h * D:(h + 1) * D] for h in range(H)], axis=0)

    def column(a, first_lane):
        return jnp.concatenate(
            [jnp.sum(jnp.where(lane == first_lane + h, a, 0.0), axis=1, keepdims=True)
             for h in range(H)], axis=0)

    def setup(n):
        r0 = pl.multiple_of(n * C, C)
        q = l2norm(stack(conv_silu(0, r0, n))) * (D ** -0.5)
        k = l2norm(stack(conv_silu(1, r0, n)))
        v = stack(conv_silu(2, r0, n))
        gt = gt_ref[pl.ds(r0, C), :]
        beta = column(_sigmoid(gt), 0)
        g_cum = _mm_f32(tril, neg_a * _softplus(gt + dt_bias))
        gb = jnp.broadcast_to(column(g_cum, H), (R, LANES))
        gcol = jnp.concatenate([gb, gb], axis=1)
        decay = jnp.exp(jnp.where(causal, gcol - gcol.T, -jnp.inf))

        qkk = _mm_nt(jnp.concatenate([q, k], axis=0), k)
        x = jnp.where(strict, -(beta * qkk[R:, :] * decay), 0.0)
        return x, (q, k, v, beta, gb, qkk, decay)

    def finish(n, t, aux):
        q, k, v, beta, gb, qkk, decay = aux
        eg = jnp.exp(gb)
        uw = _mm(t, jnp.concatenate([beta * v, (beta * eg) * k], axis=1))
        g_end = jnp.concatenate(
            [jnp.broadcast_to(gb[h * C + C - 1:(h + 1) * C, :], (C, LANES)) for h in range(H)],
            axis=0)
        qeg = q * eg
        u0_ref[n] = uw[:, :D]
        for h in range(H):
            rows = slice(h * C, (h + 1) * C)
            lhs_ref[n, h] = jnp.concatenate([uw[rows, D:], qeg[rows, :]], axis=0).astype(BF16)
        qkd_ref[n] = (qkk[:R, :] * decay).astype(BF16)
        kdt_ref[n] = (k * jnp.exp(g_end - gb)).T.astype(BF16)
        gend_ref[n] = jnp.concatenate(
            [jnp.broadcast_to(jnp.exp(gb[h * C + C - 1:(h + 1) * C, :]), (8, LANES))
             for h in range(H)], axis=1)

    def prepare(i, carry):
        chunks = [i * DN_GROUP + j for j in range(DN_GROUP)]
        xs, auxs = zip(*[setup(n) for n in chunks])
        ts = [eyef + x for x in xs]
        ps = list(xs)
        for _ in range(5):
            ps = [_mm(p, p) for p in ps]
            ts = [t + _mm(t, p) for t, p in zip(ts, ps)]
        for n, t, aux in zip(chunks, ts, auxs):
            finish(n, t, aux)
        return carry

    lax.fori_loop(0, n_chunks // DN_GROUP, prepare, 0)

    def advance(n, carry):
        r0 = pl.multiple_of(n * C, C)
        state = state_ref[...]
        ws = [jnp.dot(lhs_ref[n, h], state[:, h * D:(h + 1) * D].astype(BF16),
                      preferred_element_type=F32) for h in range(H)]
        u = u0_ref[n] - jnp.concatenate([w[:C, :] for w in ws], axis=0)
        o = jnp.concatenate([w[C:, :] for w in ws], axis=0) \
            + jnp.dot(qkd_ref[n], u.astype(BF16), preferred_element_type=F32)
        u_wide = jnp.where(own_state, jnp.concatenate([u] * H, axis=1), 0.0).astype(BF16)
        state_ref[...] = state * gend_ref[n][0:1, :] \
            + jnp.dot(kdt_ref[n], u_wide, preferred_element_type=F32)
        o = _rms(o, ng) * _silu(stack(z_ref[pl.ds(r0, C), :]))
        o_ref[pl.ds(r0, C), :] = jnp.concatenate(
            [o[h * C:(h + 1) * C, :] for h in range(H)], axis=1)
        return carry

    lax.fori_loop(0, n_chunks, advance, 0, unroll=2)
    last = q_ref.shape[0] - 8
    for i in range(3):
        halo_ref[i] = srcs[i][last:, :]


def _dn_call(dn, conv_w, a_log, dt_bias, norm_g):
    b, s, _ = dn.shape
    st = min(SEQ_TILE, s)
    nc = st // CHUNK
    rows = DN_HEADS * CHUNK
    par = jnp.zeros((8, LANES), F32)
    par = par.at[0, DN_HEADS:2 * DN_HEADS].set(a_log).at[1, DN_HEADS:2 * DN_HEADS].set(dt_bias)
    col = lambda blk: pl.BlockSpec((None, st, DN_WIDTH), lambda i, j, blk=blk: (i, j, blk))
    full = lambda a: pl.BlockSpec(a.shape, lambda i, j: (0, 0))
    ng = norm_g.reshape(1, LANES)
    return pl.pallas_call(
        _dn_kernel,
        grid=(b, s // st),
        in_specs=[col(0), col(1), col(2), col(3),
                  pl.BlockSpec((None, st, LANES), lambda i, j: (i, j, 4 * DN_HEADS)),
                  full(conv_w), full(par), full(ng)],
        out_specs=pl.BlockSpec((None, st, DN_WIDTH), lambda i, j: (i, j, 0)),
        out_shape=jax.ShapeDtypeStruct((b, s, DN_WIDTH), F32),
        scratch_shapes=[pltpu.VMEM((DN_DHEAD, DN_WIDTH), F32),
                        pltpu.VMEM((3, 8, DN_WIDTH), F32),
                        pltpu.VMEM((nc, rows, DN_DHEAD), F32),
                        pltpu.VMEM((nc, DN_HEADS, 2 * CHUNK, DN_DHEAD), BF16),
                        pltpu.VMEM((nc, rows, rows), BF16),
                        pltpu.VMEM((nc, DN_DHEAD, rows), BF16),
                        pltpu.VMEM((nc, 8, DN_WIDTH), F32)],
        compiler_params=pltpu.CompilerParams(dimension_semantics=("arbitrary", "arbitrary"),
                                             vmem_limit_bytes=VMEM_LIMIT),
        name="deltanet",
    )(dn, dn, dn, dn, dn, conv_w, par, ng)


def _hg_kernel(q_ref, f_ref, i_ref, g_ref, lb_ref, ng_ref, o_ref, state_ref, intra_ref,
               qdec_ref, kdec_ref, vt_ref, gend_ref, *, layer):
    C = CHUNK
    n_chunks = q_ref.shape[0] // C
    npair = HG_WIDTH // LANES
    nsub = C // HG_SUB
    lane = lax.broadcasted_iota(jnp.int32, (C, LANES), 1)
    row = lax.broadcasted_iota(jnp.int32, (C, LANES), 0)
    r64 = lax.broadcasted_iota(jnp.int32, (C, C), 0)
    c64 = lax.broadcasted_iota(jnp.int32, (C, C), 1)
    tril = (c64 <= r64).astype(F32)
    rl = lax.broadcasted_iota(jnp.int32, (LANES, LANES), 0)
    cl = lax.broadcasted_iota(jnp.int32, (LANES, LANES), 1)
    same_head = (rl < HG_DHEAD) == (cl < HG_DHEAD)
    head_ones = same_head.astype(BF16)
    head_mean = same_head.astype(F32) * (1.0 / HG_DHEAD)
    sub_row = lax.broadcasted_iota(jnp.int32, (HG_SUB, LANES), 0)
    pairs = []
    size = HG_SUB
    while size < C:
        pairs += [(first, size) for first in range(size, C, 2 * size)]
        size *= 2
    head0 = (lax.broadcasted_iota(jnp.int32, (C, LANES * len(pairs)), 1) & (LANES - 1)) < HG_DHEAD

    @pl.when(pl.program_id(1) == 0)
    def _():
        state_ref[...] = jnp.zeros_like(state_ref)

    lbr = lb_ref[...]
    e = jnp.exp(lbr - jnp.max(lbr, axis=0, keepdims=True))
    sm = e / jnp.sum(e, axis=0, keepdims=True)
    lb_all = jnp.zeros((1, HG_WIDTH), F32)
    for i in range(1, layer + 1):
        lb_all = lb_all + sm[i:i + 1, :]

    def load(unit):
        n, pr = unit
        r0 = pl.multiple_of(n * C, C)
        sl = slice(pr * LANES, (pr + 1) * LANES)
        lb = lb_all[:, sl]
        fg = lb + (1.0 - lb) * _sigmoid(f_ref[pl.ds(r0, C), sl])
        qc = _silu(q_ref[pl.ds(r0, C), sl])
        vc = i_ref[pl.ds(r0, C), sl]
        bc = _mm_f32(tril, jnp.log(fg))
        return qc, 1.0 - fg, vc, bc

    def diag_products(vals):
        qc, kc, vc, bc = vals
        parts = []
        for j in range(nsub):
            lo = j * HG_SUB
            bblk = bc[lo:lo + HG_SUB, :]
            qblk = qc[lo:lo + HG_SUB, :]
            for s in range(HG_SUB):
                dec = jnp.exp(jnp.where(sub_row >= s, bblk - bc[lo + s:lo + s + 1, :], -jnp.inf))
                parts.append((qblk * kc[lo + s:lo + s + 1, :] * dec).astype(BF16))
        return jnp.dot(jnp.concatenate(parts, axis=0), head_ones, preferred_element_type=F32)

    def off_diag_scores(vals):
        qc, kc, vc, bc = vals
        qts, kts = [], []
        for first, size in pairs:
            bref = bc[first:first + 1, :]
            qmask = (row >= first) & (row < first + size)
            kmask = (row >= first - size) & (row < first)
            qts.append(jnp.where(qmask, qc * jnp.exp(jnp.where(qmask, bc - bref, 0.0)), 0.0))
            kts.append(jnp.where(kmask, kc * jnp.exp(jnp.where(kmask, bref - bc, 0.0)), 0.0))
        qt = jnp.concatenate(qts, axis=1)
        kt = jnp.concatenate(kts, axis=1).astype(BF16)
        return _mm_nt(jnp.where(head0, qt, 0.0), kt), _mm_nt(jnp.where(head0, 0.0, qt), kt)

    def store(unit, vals, rs, a01):
        n, pr = unit
        qc, kc, vc, bc = vals
        diag = []
        for j in range(nsub):
            lo = j * HG_SUB
            acc = jnp.zeros((HG_SUB, LANES), F32)
            for s in range(HG_SUB):
                base = (j * HG_SUB + s) * HG_SUB
                acc = acc + rs[base:base + HG_SUB, :] * vc[lo + s:lo + s + 1, :]
            diag.append(acc)
        b_end = bc[C - 1:C, :]
        intra_ref[n, pr] = jnp.concatenate(diag, axis=0) \
            + jnp.where(lane < HG_DHEAD, _mm(a01[0], vc), _mm(a01[1], vc))
        qdec_ref[n, pr] = (qc * jnp.exp(bc)).astype(BF16)
        kdec_ref[n, pr] = (kc * jnp.exp(b_end - bc)).astype(BF16)
        vt_ref[n, pr] = vc.T.astype(BF16)
        gend_ref[n, pr] = jnp.broadcast_to(jnp.exp(b_end), (8, LANES))

    def prepare(i, carry):
        units = [(i * HG_GROUP + c, pr) for c in range(HG_GROUP) for pr in range(npair)]
        vals = [load(u) for u in units]
        rss = [diag_products(v) for v in vals]
        a01s = [off_diag_scores(v) for v in vals]
        for u, v, rs, a01 in zip(units, vals, rss, a01s):
            store(u, v, rs, a01)
        return carry

    lax.fori_loop(0, n_chunks // HG_GROUP, prepare, 0)

    def advance(i, carry):
        chunks = [i * HG_GROUP + c for c in range(HG_GROUP)]
        states = []
        for pr in range(npair):
            seq = [state_ref[pr]]
            for n in chunks:
                seq.append(jnp.where(
                    same_head,
                    seq[-1] * gend_ref[n, pr][0:1, :]
                    + jnp.dot(vt_ref[n, pr], kdec_ref[n, pr], preferred_element_type=F32), 0.0))
            state_ref[pr] = seq[-1]
            states.append(seq)
        os = [[intra_ref[n, pr] + lax.dot_general(
            qdec_ref[n, pr], states[pr][c].astype(BF16), (((1,), (1,)), ((), ())),
            preferred_element_type=F32) for pr in range(npair)] for c, n in enumerate(chunks)]
        mss = [[_mm_f32(o * o, head_mean) for o in row_os] for row_os in os]
        for c, n in enumerate(chunks):
            r0 = pl.multiple_of(n * C, C)
            o_ref[pl.ds(r0, C), :] = jnp.concatenate(
                [os[c][pr] * lax.rsqrt(mss[c][pr] + RMS_EPS) * ng_ref[...]
                 * _silu(g_ref[pl.ds(r0, C), pr * LANES:(pr + 1) * LANES])
                 for pr in range(npair)], axis=1)
        return carry

    lax.fori_loop(0, n_chunks // HG_GROUP, advance, 0)


def _hg_call(hg, hg_lb, norm_g, layer):
    b, s, _ = hg.shape
    st = min(SEQ_TILE, s)
    nc = st // CHUNK
    npair = HG_WIDTH // LANES
    col = lambda blk: pl.BlockSpec((None, st, HG_WIDTH), lambda i, j, blk=blk: (i, j, blk))
    full = lambda a: pl.BlockSpec(a.shape, lambda i, j: (0, 0))
    ng2 = jnp.tile(norm_g, LANES // HG_DHEAD).reshape(1, LANES)
    return pl.pallas_call(
        functools.partial(_hg_kernel, layer=layer),
        grid=(b, s // st),
        in_specs=[col(0), col(1), col(2), col(3), full(hg_lb), full(ng2)],
        out_specs=pl.BlockSpec((None, st, HG_WIDTH), lambda i, j: (i, j, 0)),
        out_shape=jax.ShapeDtypeStruct((b, s, HG_WIDTH), F32),
        scratch_shapes=[pltpu.VMEM((npair, LANES, LANES), F32),
                        pltpu.VMEM((nc, npair, CHUNK, LANES), F32),
                        pltpu.VMEM((nc, npair, CHUNK, LANES), BF16),
                        pltpu.VMEM((nc, npair, CHUNK, LANES), BF16),
                        pltpu.VMEM((nc, npair, LANES, CHUNK), BF16),
                        pltpu.VMEM((nc, npair, 8, LANES), F32)],
        compiler_params=pltpu.CompilerParams(dimension_semantics=("arbitrary", "arbitrary"),
                                             vmem_limit_bytes=VMEM_LIMIT),
        name="hgrn2",
    )(hg, hg, hg, hg, hg_lb, ng2)


def _sa_kernel(q_ref, qi_ref, wq_ref, kv_ref, kiw_ref, o_ref, kvb_ref, kvt_ref, kib_ref,
               key_ref, bias_ref, acc_ref, tri_ref, *, topk):
    T = Q_BLOCK
    TK = SA_KEY_TILE
    NACC = 4
    s_len = kv_ref.shape[0]
    j = pl.program_id(1)
    nkt = (j * T + T - 1) // TK + 1
    row_k = lax.broadcasted_iota(jnp.int32, (TK, T), 0)
    qpos = j * T + lax.broadcasted_iota(jnp.int32, (1, T), 1)
    first_hidden = ((qpos >> CHUNK_SHIFT) + 1) << CHUNK_SHIFT

    @pl.when(j == 0)
    def _():
        def prep(t, c):
            r0 = pl.multiple_of(t * TK, TK)
            kv = kv_ref[pl.ds(r0, TK), :]
            kvb_ref[pl.ds(r0, TK), :] = kv.astype(BF16)
            kvt_ref[:, pl.ds(r0, TK)] = kv.T.astype(BF16)
            kib_ref[pl.ds(r0, TK), :] = kiw_ref[pl.ds(r0, TK), :].astype(BF16)
            return c
        lax.fori_loop(0, s_len // TK, prep, 0)
        tri_ref[...] = jnp.where(lax.broadcasted_iota(jnp.int32, (TK, TK), 1)
                                 <= lax.broadcasted_iota(jnp.int32, (TK, TK), 0),
                                 1.0, 0.0).astype(BF16)

    w_t = wq_ref[...].T * (IDX_HEADS ** -0.5 * IDX_DHEAD ** -0.5)
    w_all = jnp.concatenate(
        [w_t[IDX_DHEAD + hh:IDX_DHEAD + hh + 1, :] for hh in range(IDX_HEADS)], axis=1)

    def stack_heads(ref):
        return jnp.concatenate(
            [ref[:, hh * LANES:(hh + 1) * LANES] for hh in range(ref.shape[1] // LANES)], axis=0)
    qi_all = stack_heads(qi_ref).astype(BF16)

    n_pairs = nkt // 2
    odd = nkt % 2 == 1

    def score_tiles(kts):
        r0s = [pl.multiple_of(kt * TK, TK) for kt in kts]
        ds = [_mm_nt(kib_ref[pl.ds(r0, TK), :], qi_all) for r0 in r0s]
        for kt, r0, d in zip(kts, r0s, ds):
            d = jnp.maximum(d, 0.0) * w_all
            sc = d[:, :T]
            for hh in range(1, IDX_HEADS):
                sc = sc + d[:, hh * T:(hh + 1) * T]
            sc = jnp.where(sc == 0.0, 0.0, sc)
            bits = lax.bitcast_convert_type(sc, jnp.int32)
            key = jnp.where(bits < 0, bits ^ 0x7FFFFFFF, bits)
            key_ref[pl.ds(r0, TK), :] = jnp.where(kt * TK + row_k >= first_hidden, INT_MIN, key)

    def score_pair(i, c):
        score_tiles([2 * i, 2 * i + 1])
        return c
    lax.fori_loop(0, n_pairs, score_pair, 0)

    @pl.when(odd)
    def _():
        score_tiles([nkt - 1])

    def count(pred):
        def f(kt, acc):
            r0 = pl.multiple_of(kt * TK, TK)
            m = pred(key_ref[pl.ds(r0, TK), :], kt * TK + row_k).astype(jnp.int32)
            return acc + jnp.sum(m.reshape(TK // (8 * NACC), NACC, 8, T), axis=0)
        acc = lax.fori_loop(0, nkt, f, jnp.zeros((NACC, 8, T), jnp.int32))
        return jnp.sum(jnp.sum(acc, axis=0), axis=0, keepdims=True)

    c0 = count(lambda k, i: k >= 0)
    nonneg = c0 >= topk
    thr0 = jnp.where(nonneg, 0, INT_MIN).astype(jnp.int32)
    n_gt0 = jnp.where(nonneg, 0, c0)

    def bit_step(i, carry):
        thr, n_ge, n_gt = carry
        cand = thr + jnp.left_shift(jnp.int32(1), 30 - i)
        c = count(lambda k, idx: k >= cand)
        take = c >= topk
        return jnp.where(take, cand, thr), jnp.where(take, c, n_ge), jnp.where(take, n_gt, c)
    thr, n_ge, n_gt = lax.fori_loop(0, 31, bit_step, (thr0, c0, n_gt0))

    has_ties = jnp.max(((n_ge > topk) & (thr > INT_MIN)).astype(jnp.int32)) > 0

    def bias_with_ties():
        need = jnp.where(thr > INT_MIN, topk - n_gt, 0).astype(F32)

        def bias_tile(kt, seen):
            r0 = pl.multiple_of(kt * TK, TK)
            k = key_ref[pl.ds(r0, TK), :]
            tie = k == thr
            rank = seen + jnp.dot(tri_ref[...], jnp.where(tie, 1.0, 0.0).astype(BF16),
                                  preferred_element_type=F32)
            take = jnp.where(tie, rank, jnp.inf) <= need
            bias_ref[pl.ds(r0, TK), :] = jnp.where(k > thr, 0.0, jnp.where(take, 0.0, -jnp.inf))
            return rank[TK - 1:TK, :]
        lax.fori_loop(0, nkt, bias_tile, jnp.zeros((1, T), F32))

    def bias_without_ties():
        thr_lo = jnp.maximum(thr, INT_MIN + 1)

        def bias_tile(kt, c):
            r0 = pl.multiple_of(kt * TK, TK)
            bias_ref[pl.ds(r0, TK), :] = jnp.where(key_ref[pl.ds(r0, TK), :] >= thr_lo,
                                                   0.0, -jnp.inf)
            return c
        lax.fori_loop(0, nkt, bias_tile, 0)
    lax.cond(has_ties, bias_with_ties, bias_without_ties)

    acc_ref[...] = jnp.zeros_like(acc_ref)
    q_all = (stack_heads(q_ref) * (SA_DHEAD ** -0.5)).astype(BF16)

    def att_tiles(kts, carry):
        m_old, den = carry
        r0s = [pl.multiple_of(kt * TK, TK) for kt in kts]
        lgs = [_mm_nt(kvb_ref[pl.ds(r0, TK), :], q_all) for r0 in r0s]
        lgs = [lg + jnp.concatenate([bias_ref[pl.ds(r0, TK), :]] * SA_HEADS, axis=1)
               for lg, r0 in zip(lgs, r0s)]
        m_new = jnp.maximum(m_old, jnp.max(functools.reduce(jnp.maximum, lgs), axis=0,
                                           keepdims=True))
        m_use = jnp.where(m_new == -jnp.inf, 0.0, m_new)
        alpha = jnp.exp(m_old - m_use)
        ps = [jnp.exp(lg - m_use) for lg in lgs]
        pvs = [jnp.dot(kvt_ref[:, pl.ds(r0, TK)], p.astype(BF16), preferred_element_type=F32)
               for p, r0 in zip(ps, r0s)]
        acc_ref[...] = acc_ref[...] * alpha + sum(pvs[1:], pvs[0])
        return m_new, den * alpha + jnp.sum(sum(ps[1:], ps[0]), axis=0, keepdims=True)
    hq = SA_HEADS * T
    carry = lax.fori_loop(0, n_pairs, lambda i, c: att_tiles([2 * i, 2 * i + 1], c),
                          (jnp.full((1, hq), -jnp.inf, F32), jnp.zeros((1, hq), F32)))
    _, den = lax.cond(odd, lambda c: att_tiles([nkt - 1], c), lambda c: c, carry)
    res = acc_ref[...] / den
    for hh in range(SA_HEADS):
        o_ref[:, hh * SA_DHEAD:(hh + 1) * SA_DHEAD] = res[:, hh * T:(hh + 1) * T].T[:, SA_DHEAD:]


def _sa_call(sa):
    b, s, _ = sa.shape
    topk = min(IDX_TOPK_MAX, s // 4)
    T = Q_BLOCK
    assert s % SA_KEY_TILE == 0
    qw = SA_HEADS * LANES
    seq = lambda blk: pl.BlockSpec((None, s, LANES), lambda i, j, blk=blk: (i, 0, blk))
    return pl.pallas_call(
        functools.partial(_sa_kernel, topk=topk),
        grid=(b, s // T),
        in_specs=[pl.BlockSpec((None, T, qw), lambda i, j: (i, j, 0)),
                  pl.BlockSpec((None, T, qw), lambda i, j: (i, j, 1)),
                  pl.BlockSpec((None, T, LANES), lambda i, j: (i, j, 2 * SA_HEADS + 1)),
                  seq(2 * SA_HEADS), seq(2 * SA_HEADS + 1)],
        out_specs=pl.BlockSpec((None, T, SA_WIDTH), lambda i, j: (i, j, 0)),
        out_shape=jax.ShapeDtypeStruct((b, s, SA_WIDTH), F32),
        scratch_shapes=[pltpu.VMEM((s, LANES), BF16), pltpu.VMEM((LANES, s), BF16),
                        pltpu.VMEM((s, LANES), BF16), pltpu.VMEM((s, T), jnp.int32),
                        pltpu.VMEM((s, T), F32), pltpu.VMEM((LANES, SA_HEADS * T), F32),
                        pltpu.VMEM((SA_KEY_TILE, SA_KEY_TILE), BF16)],
        compiler_params=pltpu.CompilerParams(dimension_semantics=("arbitrary", "arbitrary"),
                                             vmem_limit_bytes=VMEM_LIMIT),
        name="dsa",
    )(sa, sa, sa, sa, sa)


def _mix_ffn_kernel(x_ref, oa_ref, ob_ref, oc_ref, wo_ref, gpost_ref, gpre_ref,
                    wup_ref, cw_ref, wdn_ref, gout_ref, o_ref, carry_ref, act_ref, *,
                    tiles_per_seq):
    tm = x_ref.shape[0]
    d_ff = wdn_ref.shape[0]
    first = pl.program_id(0) % tiles_per_seq == 0
    mixed = jnp.concatenate([oa_ref[...], ob_ref[...], oc_ref[...]], axis=1).astype(BF16)
    mix = jnp.dot(mixed, wo_ref[...], preferred_element_type=F32)
    x1 = x_ref[...] + _rms(mix, gpost_ref[...])
    h = _rms(x1, gpre_ref[...]).astype(BF16)
    row = lax.broadcasted_iota(jnp.int32, (tm, FF_TILE), 0)

    def conv(c0):
        u = jnp.dot(h, wup_ref[:, c0:c0 + FF_TILE], preferred_element_type=F32)
        prev = jnp.where(first, 0.0, carry_ref[:, c0:c0 + FF_TILE])
        carry_ref[:, c0:c0 + FF_TILE] = u[tm - 8:, :]
        u1 = jnp.where(row == 0, prev[7:8, :], pltpu.roll(u, 1, 0))
        u2 = jnp.where(row == 0, prev[6:7, :],
                       jnp.where(row == 1, prev[7:8, :], pltpu.roll(u, 2, 0)))
        w = cw_ref[:, c0:c0 + FF_TILE]
        return u2 * w[0:1, :] + u1 * w[1:2, :] + u * w[2:3, :]

    for c in range(d_ff // FF_TILE):
        gate = conv(c * FF_TILE)
        val = conv(d_ff + c * FF_TILE)
        inner = gate + 0.044715 * (gate * gate * gate)
        act = 0.5 * gate * (1.0 + jnp.tanh(0.7978845608028654 * inner)) * val
        act_ref[:, c * FF_TILE:(c + 1) * FF_TILE] = act.astype(BF16)
    down = jnp.dot(act_ref[...], wdn_ref[...], preferred_element_type=F32)
    o_ref[...] = x1 + _rms(down, gout_ref[...])


def _mix_ffn_call(x2, oa, ob, oc, wo, g_post, g_pre, w_up, conv_w, w_down, g_out, seq,
                  tm=512):
    m, d = x2.shape
    rows = lambda a: pl.BlockSpec((tm, a.shape[1]), lambda i: (i, 0))
    full = lambda a: pl.BlockSpec(a.shape, lambda i: (0, 0), pipeline_mode=pl.Buffered(1))
    return pl.pallas_call(
        functools.partial(_mix_ffn_kernel, tiles_per_seq=seq // tm),
        grid=(m // tm,),
        in_specs=[rows(x2), rows(oa), rows(ob), rows(oc), full(wo),
                  full(g_post), full(g_pre), full(w_up), full(conv_w), full(w_down), full(g_out)],
        out_specs=pl.BlockSpec((tm, d), lambda i: (i, 0)),
        out_shape=jax.ShapeDtypeStruct((m, d), F32),
        scratch_shapes=[pltpu.VMEM((8, w_up.shape[1]), F32),
                        pltpu.VMEM((tm, w_down.shape[0]), BF16)],
        compiler_params=pltpu.CompilerParams(dimension_semantics=("arbitrary",),
                                             vmem_limit_bytes=VMEM_LIMIT),
        name="mix_ffn",
    )(x2, oa, ob, oc, wo, g_post, g_pre, w_up, conv_w, w_down, g_out)


def kernel(x, w_in, dn_conv, dn_a_log, dn_dt_bias, dn_norm, hg_lb, hg_norm, w_out,
           g_mix_pre, g_mix_post, g_ffn_pre, g_ffn_post, ffn_w_up, ffn_conv, ffn_w_down):
    b, s, d = x.shape
    depth = w_in.shape[0]
    x2 = x.reshape(b * s, d)
    for l in range(depth):
        dn, sa, hg = _proj_call(x2, g_mix_pre[l].reshape(1, d), w_in[l].astype(BF16))
        o_a = _dn_call(dn.reshape(b, s, -1), dn_conv[l], dn_a_log[l], dn_dt_bias[l], dn_norm[l])
        o_b = _sa_call(sa.reshape(b, s, -1))
        o_c = _hg_call(hg.reshape(b, s, -1), hg_lb, hg_norm[l], l)
        x2 = _mix_ffn_call(
            x2, o_a.reshape(b * s, -1), o_b.reshape(b * s, -1), o_c.reshape(b * s, -1),
            w_out[l].astype(BF16),
            g_mix_post[l].reshape(1, d), g_ffn_pre[l].reshape(1, d),
            ffn_w_up[l].astype(BF16), ffn_conv[l], ffn_w_down[l].astype(BF16),
            g_ffn_post[l].reshape(1, d), s)
    return x2.reshape(b, s, d)
```

```python
import functools

import jax
import jax.numpy as jnp
from jax import lax
from jax.experimental import pallas as pl
from jax.experimental.pallas import tpu as pltpu

F32 = jnp.float32
BF16 = jnp.bfloat16
HIGHEST = lax.Precision.HIGHEST

LANES = 128
CHUNK = 64
CHUNK_SHIFT = 6
RMS_EPS = 1e-6
DN_DHEAD = 128
DN_HEADS = 4
DN_WIDTH = DN_HEADS * DN_DHEAD
DN_CONV = 4
SA_DHEAD = 64
SA_HEADS = 4
SA_WIDTH = SA_HEADS * SA_DHEAD
IDX_HEADS = 4
IDX_DHEAD = 64
IDX_TOPK_MAX = 256
Q_BLOCK = 256
SA_KEY_TILE = 512
HG_DHEAD = 64
HG_HEADS = 4
HG_WIDTH = HG_HEADS * HG_DHEAD
HG_SUB = 16
FFN_CONV = 3
FF_TILE = 256
SEQ_TILE = 1024
DN_GROUP = 4
HG_GROUP = 4
INT_MIN = -2 ** 31

VMEM_LIMIT = 56 * 1024 * 1024

DN_DHEAD_SHIFT = 7
assert CHUNK == 1 << CHUNK_SHIFT and DN_DHEAD == 1 << DN_DHEAD_SHIFT


def _mm(a, b):
    return jnp.dot(a.astype(BF16), b.astype(BF16), preferred_element_type=F32)


def _mm_nt(a, b):
    return lax.dot_general(a.astype(BF16), b.astype(BF16), (((1,), (1,)), ((), ())),
                           preferred_element_type=F32)


def _mm_f32(a, b):
    return jnp.dot(a, b, precision=HIGHEST, preferred_element_type=F32)


def _sigmoid(x):
    return 1.0 / (1.0 + jnp.exp(-x))


def _silu(x):
    return x * _sigmoid(x)


def _softplus(x):
    return jnp.maximum(x, 0.0) + jnp.log1p(jnp.exp(-jnp.abs(x)))


def _rms(x, g):
    return x * lax.rsqrt(jnp.mean(x * x, axis=-1, keepdims=True) + RMS_EPS) * g


DN_COLS = 4 * DN_WIDTH + LANES
SA_COLS = 2 * SA_HEADS * LANES + 2 * LANES
HG_COLS = 4 * HG_WIDTH


def _proj_kernel(x_ref, g_ref, w_ref, od_ref, os_ref, oh_ref):
    h = _rms(x_ref[...], g_ref[...]).astype(BF16)
    y = jnp.dot(h, w_ref[...], preferred_element_type=F32)
    zeros = lambda n: jnp.zeros((y.shape[0], n), F32)
    pos = [0]

    def take(n):
        piece = y[:, pos[0]:pos[0] + n]
        pos[0] += n
        return piece

    def padded_heads(n_heads, width):
        return [p for _ in range(n_heads) for p in (take(width), zeros(LANES - width))]

    od_ref[...] = jnp.concatenate([take(4 * DN_WIDTH + 2 * DN_HEADS), zeros(LANES - 2 * DN_HEADS)],
                                  axis=1)
    b_q = padded_heads(SA_HEADS, SA_DHEAD)
    b_kv = [take(2 * SA_DHEAD)]
    b_qi = padded_heads(IDX_HEADS, IDX_DHEAD)
    b_kiw = [take(IDX_DHEAD + IDX_HEADS), zeros(LANES - IDX_DHEAD - IDX_HEADS)]
    os_ref[...] = jnp.concatenate(b_q + b_qi + b_kv + b_kiw, axis=1)
    oh_ref[...] = take(HG_COLS)


def _proj_call(x2, g, w, tm=512):
    m, d = x2.shape
    full = lambda a: pl.BlockSpec(a.shape, lambda i: (0, 0), pipeline_mode=pl.Buffered(1))
    rows = lambda n: pl.BlockSpec((tm, n), lambda i: (i, 0))
    widths = (DN_COLS, SA_COLS, HG_COLS)
    return pl.pallas_call(
        _proj_kernel,
        grid=(m // tm,),
        in_specs=[rows(d), full(g), full(w)],
        out_specs=[rows(n) for n in widths],
        out_shape=[jax.ShapeDtypeStruct((m, n), F32) for n in widths],
        compiler_params=pltpu.CompilerParams(dimension_semantics=("arbitrary",),
                                             vmem_limit_bytes=VMEM_LIMIT),
        name="proj",
    )(x2, g, w)


def _dn_kernel(q_ref, k_ref, v_ref, z_ref, gt_ref, cw_ref, par_ref, ng_ref, o_ref,
               state_ref, halo_ref, u0_ref, lhs_ref, qkd_ref, kdt_ref, gend_ref):
    C = CHUNK
    D = DN_DHEAD
    H = DN_HEADS
    R = H * C
    n_chunks = q_ref.shape[0] // C
    lane = lax.broadcasted_iota(jnp.int32, (C, LANES), 1)
    row8 = lax.broadcasted_iota(jnp.int32, (8, DN_WIDTH), 0)
    r64 = lax.broadcasted_iota(jnp.int32, (C, C), 0)
    c64 = lax.broadcasted_iota(jnp.int32, (C, C), 1)
    tril = (c64 <= r64).astype(F32)
    rr = lax.broadcasted_iota(jnp.int32, (R, R), 0)
    cc = lax.broadcasted_iota(jnp.int32, (R, R), 1)
    same = (rr >> CHUNK_SHIFT) == (cc >> CHUNK_SHIFT)
    causal = same & (cc <= rr)
    strict = same & (cc < rr)
    eyef = (cc == rr).astype(F32)
    own_state = (lax.broadcasted_iota(jnp.int32, (R, H * D), 0) >> CHUNK_SHIFT) == \
        (lax.broadcasted_iota(jnp.int32, (R, H * D), 1) >> DN_DHEAD_SHIFT)
    neg_a = -jnp.exp(par_ref[0:1, :])
    dt_bias = par_ref[1:2, :]
    ng = ng_ref[...]
    srcs = (q_ref, k_ref, v_ref)

    @pl.when(pl.program_id(1) == 0)
    def _():
        state_ref[...] = jnp.zeros_like(state_ref)
        halo_ref[...] = jnp.zeros_like(halo_ref)

    def conv_silu(i, r0, n):
        ref = srcs[i]
        w = cw_ref[:, i * DN_WIDTH:(i + 1) * DN_WIDTH]
        cur = ref[pl.ds(r0, C), :]
        rp = pl.multiple_of(jnp.maximum(r0 - 8, 0), 8)
        prev = jnp.where(n > 0, ref[pl.ds(rp, 8), :], halo_ref[i])
        acc = cur * w[DN_CONV - 1:DN_CONV, :]
        for d in range(1, DN_CONV):
            rolled = pltpu.roll(cur, d, 0)
            head = jnp.where(row8 < d, pltpu.roll(prev, d, 0), rolled[:8, :])
            sh = jnp.concatenate([head, rolled[8:, :]], axis=0)
            acc = acc + sh * w[DN_CONV - 1 - d:DN_CONV - d, :]
        return _silu(acc)

    def l2norm(x):
        return x * lax.rsqrt(jnp.sum(x * x, axis=-1, keepdims=True) + 1e-6)

    def stack(a):
        return jnp.concatenate([a[:, h * D:(h + 1) * D] for h in range(H)], axis=0)

    def column(a, first_lane):
        return jnp.concatenate(
            [jnp.sum(jnp.where(lane == first_lane + h, a, 0.0), axis=1, keepdims=True)
             for h in range(H)], axis=0)

    def setup(n):
        r0 = pl.multiple_of(n * C, C)
        q = l2norm(stack(conv_silu(0, r0, n))) * (D ** -0.5)
        k = l2norm(stack(conv_silu(1, r0, n)))
        v = stack(conv_silu(2, r0, n))
        gt = gt_ref[pl.ds(r0, C), :]
        beta = column(_sigmoid(gt), 0)
        g_cum = _mm_f32(tril, neg_a * _softplus(gt + dt_bias))
        gb = jnp.broadcast_to(column(g_cum, H), (R, LANES))
        gcol = jnp.concatenate([gb, gb], axis=1)
        decay = jnp.exp(jnp.where(causal, gcol - gcol.T, -jnp.inf))

        qkk = _mm_nt(jnp.concatenate([q, k], axis=0), k)
        x = jnp.where(strict, -(beta * qkk[R:, :] * decay), 0.0)
        return x, (q, k, v, beta, gb, qkk, decay)

    def finish(n, t, aux):
        q, k, v, beta, gb, qkk, decay = aux
        eg = jnp.exp(gb)
        uw = _mm(t, jnp.concatenate([beta * v, (beta * eg) * k], axis=1))
        g_end = jnp.concatenate(
            [jnp.broadcast_to(gb[h * C + C - 1:(h + 1) * C, :], (C, LANES)) for h in range(H)],
            axis=0)
        qeg = q * eg
        u0_ref[n] = uw[:, :D]
        for h in range(H):
            rows = slice(h * C, (h + 1) * C)
            lhs_ref[n, h] = jnp.concatenate([uw[rows, D:], qeg[rows, :]], axis=0).astype(BF16)
        qkd_ref[n] = (qkk[:R, :] * decay).astype(BF16)
        kdt_ref[n] = (k * jnp.exp(g_end - gb)).T.astype(BF16)
        gend_ref[n] = jnp.concatenate(
            [jnp.broadcast_to(jnp.exp(gb[h * C + C - 1:(h + 1) * C, :]), (8, LANES))
             for h in range(H)], axis=1)

    def prepare(i, carry):
        chunks = [i * DN_GROUP + j for j in range(DN_GROUP)]
        xs, auxs = zip(*[setup(n) for n in chunks])
        ts = [eyef + x for x in xs]
        ps = list(xs)
        for _ in range(5):
            ps = [_mm(p, p) for p in ps]
            ts = [t + _mm(t, p) for t, p in zip(ts, ps)]
        for n, t, aux in zip(chunks, ts, auxs):
            finish(n, t, aux)
        return carry

    lax.fori_loop(0, n_chunks // DN_GROUP, prepare, 0)

    def advance(n, carry):
        r0 = pl.multiple_of(n * C, C)
        state = state_ref[...]
        ws = [jnp.dot(lhs_ref[n, h], state[:, h * D:(h + 1) * D].astype(BF16),
                      preferred_element_type=F32) for h in range(H)]
        u = u0_ref[n] - jnp.concatenate([w[:C, :] for w in ws], axis=0)
        o = jnp.concatenate([w[C:, :] for w in ws], axis=0) \
            + jnp.dot(qkd_ref[n], u.astype(BF16), preferred_element_type=F32)
        u_wide = jnp.where(own_state, jnp.concatenate([u] * H, axis=1), 0.0).astype(BF16)
        state_ref[...] = state * gend_ref[n][0:1, :] \
            + jnp.dot(kdt_ref[n], u_wide, preferred_element_type=F32)
        o = _rms(o, ng) * _silu(stack(z_ref[pl.ds(r0, C), :]))
        o_ref[pl.ds(r0, C), :] = jnp.concatenate(
            [o[h * C:(h + 1) * C, :] for h in range(H)], axis=1)
        return carry

    lax.fori_loop(0, n_chunks, advance, 0, unroll=2)
    last = q_ref.shape[0] - 8
    for i in range(3):
        halo_ref[i] = srcs[i][last:, :]


def _dn_call(dn, conv_w, a_log, dt_bias, norm_g):
    b, s, _ = dn.shape
    st = min(SEQ_TILE, s)
    nc = st // CHUNK
    rows = DN_HEADS * CHUNK
    par = jnp.zeros((8, LANES), F32)
    par = par.at[0, DN_HEADS:2 * DN_HEADS].set(a_log).at[1, DN_HEADS:2 * DN_HEADS].set(dt_bias)
    col = lambda blk: pl.BlockSpec((None, st, DN_WIDTH), lambda i, j, blk=blk: (i, j, blk))
    full = lambda a: pl.BlockSpec(a.shape, lambda i, j: (0, 0))
    ng = norm_g.reshape(1, LANES)
    return pl.pallas_call(
        _dn_kernel,
        grid=(b, s // st),
        in_specs=[col(0), col(1), col(2), col(3),
                  pl.BlockSpec((None, st, LANES), lambda i, j: (i, j, 4 * DN_HEADS)),
                  full(conv_w), full(par), full(ng)],
        out_specs=pl.BlockSpec((None, st, DN_WIDTH), lambda i, j: (i, j, 0)),
        out_shape=jax.ShapeDtypeStruct((b, s, DN_WIDTH), F32),
        scratch_shapes=[pltpu.VMEM((DN_DHEAD, DN_WIDTH), F32),
                        pltpu.VMEM((3, 8, DN_WIDTH), F32),
                        pltpu.VMEM((nc, rows, DN_DHEAD), F32),
                        pltpu.VMEM((nc, DN_HEADS, 2 * CHUNK, DN_DHEAD), BF16),
                        pltpu.VMEM((nc, rows, rows), BF16),
                        pltpu.VMEM((nc, DN_DHEAD, rows), BF16),
                        pltpu.VMEM((nc, 8, DN_WIDTH), F32)],
        compiler_params=pltpu.CompilerParams(dimension_semantics=("arbitrary", "arbitrary"),
                                             vmem_limit_bytes=VMEM_LIMIT),
        name="deltanet",
    )(dn, dn, dn, dn, dn, conv_w, par, ng)


def _hg_kernel(q_ref, f_ref, i_ref, g_ref, lb_ref, ng_ref, o_ref, state_ref, intra_ref,
               qdec_ref, kdec_ref, vt_ref, gend_ref, *, layer):
    C = CHUNK
    n_chunks = q_ref.shape[0] // C
    npair = HG_WIDTH // LANES
    nsub = C // HG_SUB
    lane = lax.broadcasted_iota(jnp.int32, (C, LANES), 1)
    row = lax.broadcasted_iota(jnp.int32, (C, LANES), 0)
    r64 = lax.broadcasted_iota(jnp.int32, (C, C), 0)
    c64 = lax.broadcasted_iota(jnp.int32, (C, C), 1)
    tril = (c64 <= r64).astype(F32)
    rl = lax.broadcasted_iota(jnp.int32, (LANES, LANES), 0)
    cl = lax.broadcasted_iota(jnp.int32, (LANES, LANES), 1)
    same_head = (rl < HG_DHEAD) == (cl < HG_DHEAD)
    head_ones = same_head.astype(BF16)
    head_mean = same_head.astype(F32) * (1.0 / HG_DHEAD)
    sub_row = lax.broadcasted_iota(jnp.int32, (HG_SUB, LANES), 0)
    pairs = []
    size = HG_SUB
    while size < C:
        pairs += [(first, size) for first in range(size, C, 2 * size)]
        size *= 2
    head0 = (lax.broadcasted_iota(jnp.int32, (C, LANES * len(pairs)), 1) & (LANES - 1)) < HG_DHEAD

    @pl.when(pl.program_id(1) == 0)
    def _():
        state_ref[...] = jnp.zeros_like(state_ref)

    lbr = lb_ref[...]
    e = jnp.exp(lbr - jnp.max(lbr, axis=0, keepdims=True))
    sm = e / jnp.sum(e, axis=0, keepdims=True)
    lb_all = jnp.zeros((1, HG_WIDTH), F32)
    for i in range(1, layer + 1):
        lb_all = lb_all + sm[i:i + 1, :]

    def load(unit):
        n, pr = unit
        r0 = pl.multiple_of(n * C, C)
        sl = slice(pr * LANES, (pr + 1) * LANES)
        lb = lb_all[:, sl]
        fg = lb + (1.0 - lb) * _sigmoid(f_ref[pl.ds(r0, C), sl])
        qc = _silu(q_ref[pl.ds(r0, C), sl])
        vc = i_ref[pl.ds(r0, C), sl]
        bc = _mm_f32(tril, jnp.log(fg))
        return qc, 1.0 - fg, vc, bc

    def diag_products(vals):
        qc, kc, vc, bc = vals
        parts = []
        for j in range(nsub):
            lo = j * HG_SUB
            bblk = bc[lo:lo + HG_SUB, :]
            qblk = qc[lo:lo + HG_SUB, :]
            for s in range(HG_SUB):
                dec = jnp.exp(jnp.where(sub_row >= s, bblk - bc[lo + s:lo + s + 1, :], -jnp.inf))
                parts.append((qblk * kc[lo + s:lo + s + 1, :] * dec).astype(BF16))
        return jnp.dot(jnp.concatenate(parts, axis=0), head_ones, preferred_element_type=F32)

    def off_diag_scores(vals):
        qc, kc, vc, bc = vals
        qts, kts = [], []
        for first, size in pairs:
            bref = bc[first:first + 1, :]
            qmask = (row >= first) & (row < first + size)
            kmask = (row >= first - size) & (row < first)
            qts.append(jnp.where(qmask, qc * jnp.exp(jnp.where(qmask, bc - bref, 0.0)), 0.0))
            kts.append(jnp.where(kmask, kc * jnp.exp(jnp.where(kmask, bref - bc, 0.0)), 0.0))
        qt = jnp.concatenate(qts, axis=1)
        kt = jnp.concatenate(kts, axis=1).astype(BF16)
        return _mm_nt(jnp.where(head0, qt, 0.0), kt), _mm_nt(jnp.where(head0, 0.0, qt), kt)

    def store(unit, vals, rs, a01):
        n, pr = unit
        qc, kc, vc, bc = vals
        diag = []
        for j in range(nsub):
            lo = j * HG_SUB
            acc = jnp.zeros((HG_SUB, LANES), F32)
            for s in range(HG_SUB):
                base = (j * HG_SUB + s) * HG_SUB
                acc = acc + rs[base:base + HG_SUB, :] * vc[lo + s:lo + s + 1, :]
            diag.append(acc)
        b_end = bc[C - 1:C, :]
        intra_ref[n, pr] = jnp.concatenate(diag, axis=0) \
            + jnp.where(lane < HG_DHEAD, _mm(a01[0], vc), _mm(a01[1], vc))
        qdec_ref[n, pr] = (qc * jnp.exp(bc)).astype(BF16)
        kdec_ref[n, pr] = (kc * jnp.exp(b_end - bc)).astype(BF16)
        vt_ref[n, pr] = vc.T.astype(BF16)
        gend_ref[n, pr] = jnp.broadcast_to(jnp.exp(b_end), (8, LANES))

    def prepare(i, carry):
        units = [(i * HG_GROUP + c, pr) for c in range(HG_GROUP) for pr in range(npair)]
        vals = [load(u) for u in units]
        rss = [diag_products(v) for v in vals]
        a01s = [off_diag_scores(v) for v in vals]
        for u, v, rs, a01 in zip(units, vals, rss, a01s):
            store(u, v, rs, a01)
        return carry

    lax.fori_loop(0, n_chunks // HG_GROUP, prepare, 0)

    def advance(i, carry):
        chunks = [i * HG_GROUP + c for c in range(HG_GROUP)]
        states = []
        for pr in range(npair):
            seq = [state_ref[pr]]
            for n in chunks:
                seq.append(jnp.where(
                    same_head,
                    seq[-1] * gend_ref[n, pr][0:1, :]
                    + jnp.dot(vt_ref[n, pr], kdec_ref[n, pr], preferred_element_type=F32), 0.0))
            state_ref[pr] = seq[-1]
            states.append(seq)
        os = [[intra_ref[n, pr] + lax.dot_general(
            qdec_ref[n, pr], states[pr][c].astype(BF16), (((1,), (1,)), ((), ())),
            preferred_element_type=F32) for pr in range(npair)] for c, n in enumerate(chunks)]
        mss = [[_mm_f32(o * o, head_mean) for o in row_os] for row_os in os]
        for c, n in enumerate(chunks):
            r0 = pl.multiple_of(n * C, C)
            o_ref[pl.ds(r0, C), :] = jnp.concatenate(
                [os[c][pr] * lax.rsqrt(mss[c][pr] + RMS_EPS) * ng_ref[...]
                 * _silu(g_ref[pl.ds(r0, C), pr * LANES:(pr + 1) * LANES])
                 for pr in range(npair)], axis=1)
        return carry

    lax.fori_loop(0, n_chunks // HG_GROUP, advance, 0)


def _hg_call(hg, hg_lb, norm_g, layer):
    b, s, _ = hg.shape
    st = min(SEQ_TILE, s)
    nc = st // CHUNK
    npair = HG_WIDTH // LANES
    col = lambda blk: pl.BlockSpec((None, st, HG_WIDTH), lambda i, j, blk=blk: (i, j, blk))
    full = lambda a: pl.BlockSpec(a.shape, lambda i, j: (0, 0))
    ng2 = jnp.tile(norm_g, LANES // HG_DHEAD).reshape(1, LANES)
    return pl.pallas_call(
        functools.partial(_hg_kernel, layer=layer),
        grid=(b, s // st),
        in_specs=[col(0), col(1), col(2), col(3), full(hg_lb), full(ng2)],
        out_specs=pl.BlockSpec((None, st, HG_WIDTH), lambda i, j: (i, j, 0)),
        out_shape=jax.ShapeDtypeStruct((b, s, HG_WIDTH), F32),
        scratch_shapes=[pltpu.VMEM((npair, LANES, LANES), F32),
                        pltpu.VMEM((nc, npair, CHUNK, LANES), F32),
                        pltpu.VMEM((nc, npair, CHUNK, LANES), BF16),
                        pltpu.VMEM((nc, npair, CHUNK, LANES), BF16),
                        pltpu.VMEM((nc, npair, LANES, CHUNK), BF16),
                        pltpu.VMEM((nc, npair, 8, LANES), F32)],
        compiler_params=pltpu.CompilerParams(dimension_semantics=("arbitrary", "arbitrary"),
                                             vmem_limit_bytes=VMEM_LIMIT),
        name="hgrn2",
    )(hg, hg, hg, hg, hg_lb, ng2)


def _sa_kernel(q_ref, qi_ref, wq_ref, kv_ref, kiw_ref, o_ref, kvb_ref, kvt_ref, kib_ref,
               key_ref, bias_ref, acc_ref, tri_ref, plane_ref, *, topk):
    T = Q_BLOCK
    TK = SA_KEY_TILE
    s_len = kv_ref.shape[0]
    j = pl.program_id(1)
    nkt = (j * T + T - 1) // TK + 1
    row_k = lax.broadcasted_iota(jnp.int32, (TK, T), 0)
    qpos = j * T + lax.broadcasted_iota(jnp.int32, (1, T), 1)
    first_hidden = ((qpos >> CHUNK_SHIFT) + 1) << CHUNK_SHIFT

    @pl.when(j == 0)
    def _():
        def prep(t, c):
            r0 = pl.multiple_of(t * TK, TK)
            kv = kv_ref[pl.ds(r0, TK), :]
            kvb_ref[pl.ds(r0, TK), :] = kv.astype(BF16)
            kvt_ref[:, pl.ds(r0, TK)] = kv.T.astype(BF16)
            kib_ref[pl.ds(r0, TK), :] = kiw_ref[pl.ds(r0, TK), :].astype(BF16)
            return c
        lax.fori_loop(0, s_len // TK, prep, 0)
        tri_ref[...] = jnp.where(lax.broadcasted_iota(jnp.int32, (TK, TK), 1)
                                 <= lax.broadcasted_iota(jnp.int32, (TK, TK), 0),
                                 1.0, 0.0).astype(BF16)

    w_t = wq_ref[...].T * (IDX_HEADS ** -0.5 * IDX_DHEAD ** -0.5)
    w_all = jnp.concatenate(
        [w_t[IDX_DHEAD + hh:IDX_DHEAD + hh + 1, :] for hh in range(IDX_HEADS)], axis=1)

    def stack_heads(ref):
        return jnp.concatenate(
            [ref[:, hh * LANES:(hh + 1) * LANES] for hh in range(ref.shape[1] // LANES)], axis=0)
    qi_all = stack_heads(qi_ref).astype(BF16)

    n_pairs = nkt // 2
    odd = nkt % 2 == 1

    def score_tiles(kts):
        r0s = [pl.multiple_of(kt * TK, TK) for kt in kts]
        ds = [_mm_nt(kib_ref[pl.ds(r0, TK), :], qi_all) for r0 in r0s]
        for kt, r0, d in zip(kts, r0s, ds):
            d = jnp.maximum(d, 0.0) * w_all
            sc = d[:, :T]
            for hh in range(1, IDX_HEADS):
                sc = sc + d[:, hh * T:(hh + 1) * T]
            sc = jnp.where(sc == 0.0, 0.0, sc)
            bits = lax.bitcast_convert_type(sc, jnp.int32)
            key = jnp.where(bits < 0, bits ^ 0x7FFFFFFF, bits)
            key = jnp.where(kt * TK + row_k >= first_hidden, INT_MIN, key)
            key_ref[pl.ds(r0, TK), :] = key
            words = (key ^ INT_MIN).reshape(32, TK // 32, T)
            a = [words[m] for m in range(32)]
            dist, mask = 16, 0x0000FFFF
            while dist:
                for k in range(32):
                    if not k & dist:
                        t = (a[k] ^ (a[k + dist] >> dist)) & mask
                        a[k] = a[k] ^ t
                        a[k + dist] = a[k + dist] ^ (t << dist)
                dist >>= 1
                mask ^= mask << dist
            g0 = pl.multiple_of(kt * (TK // 32), TK // 32)
            for p in range(32):
                plane_ref[p, pl.ds(g0, TK // 32), :] = a[p]

    def score_pair(i, c):
        score_tiles([2 * i, 2 * i + 1])
        return c
    lax.fori_loop(0, n_pairs, score_pair, 0)

    @pl.when(odd)
    def _():
        score_tiles([nkt - 1])

    plane_rows = s_len // 32
    tile_of_row = lax.broadcasted_iota(jnp.int32, (plane_rows, T), 0) // (TK // 32)
    alive0 = jnp.where(tile_of_row < nkt, -1, 0).astype(jnp.int32)

    def ones_in(x):
        return jnp.sum(lax.population_count(x), axis=0, keepdims=True)

    def bit_step(i, carry):
        alive, thr_u, n_gt = carry
        ones = alive & plane_ref[i]
        c1 = ones_in(ones)
        take = n_gt + c1 >= topk
        bit = jnp.left_shift(jnp.int32(1), 31 - i)
        return (jnp.where(take, ones, alive ^ ones), jnp.where(take, thr_u | bit, thr_u),
                jnp.where(take, n_gt, n_gt + c1))
    zero = jnp.zeros((1, T), jnp.int32)
    alive, thr_u, n_gt = lax.fori_loop(0, 32, bit_step, (alive0, zero, zero))
    thr = thr_u ^ INT_MIN
    n_ge = n_gt + ones_in(alive)

    has_ties = jnp.max(((n_ge > topk) & (thr > INT_MIN)).astype(jnp.int32)) > 0

    def bias_with_ties():
        need = jnp.where(thr > INT_MIN, topk - n_gt, 0).astype(F32)

        def bias_tile(kt, seen):
            r0 = pl.multiple_of(kt * TK, TK)
            k = key_ref[pl.ds(r0, TK), :]
            tie = k == thr
            rank = seen + jnp.dot(tri_ref[...], jnp.where(tie, 1.0, 0.0).astype(BF16),
                                  preferred_element_type=F32)
            take = jnp.where(tie, rank, jnp.inf) <= need
            bias_ref[pl.ds(r0, TK), :] = jnp.where(k > thr, 0.0, jnp.where(take, 0.0, -jnp.inf))
            return rank[TK - 1:TK, :]
        lax.fori_loop(0, nkt, bias_tile, jnp.zeros((1, T), F32))

    def bias_without_ties():
        thr_lo = jnp.maximum(thr, INT_MIN + 1)

        def bias_tile(kt, c):
            r0 = pl.multiple_of(kt * TK, TK)
            bias_ref[pl.ds(r0, TK), :] = jnp.where(key_ref[pl.ds(r0, TK), :] >= thr_lo,
                                                   0.0, -jnp.inf)
            return c
        lax.fori_loop(0, nkt, bias_tile, 0)
    lax.cond(has_ties, bias_with_ties, bias_without_ties)

    acc_ref[...] = jnp.zeros_like(acc_ref)
    q_all = (stack_heads(q_ref) * (SA_DHEAD ** -0.5)).astype(BF16)

    def att_tiles(kts, carry):
        m_old, den = carry
        r0s = [pl.multiple_of(kt * TK, TK) for kt in kts]
        lgs = [_mm_nt(kvb_ref[pl.ds(r0, TK), :], q_all) for r0 in r0s]
        lgs = [lg + jnp.concatenate([bias_ref[pl.ds(r0, TK), :]] * SA_HEADS, axis=1)
               for lg, r0 in zip(lgs, r0s)]
        m_new = jnp.maximum(m_old, jnp.max(functools.reduce(jnp.maximum, lgs), axis=0,
                                           keepdims=True))
        m_use = jnp.where(m_new == -jnp.inf, 0.0, m_new)
        alpha = jnp.exp(m_old - m_use)
        ps = [jnp.exp(lg - m_use) for lg in lgs]
        pvs = [jnp.dot(kvt_ref[:, pl.ds(r0, TK)], p.astype(BF16), preferred_element_type=F32)
               for p, r0 in zip(ps, r0s)]
        acc_ref[...] = acc_ref[...] * alpha + sum(pvs[1:], pvs[0])
        return m_new, den * alpha + jnp.sum(sum(ps[1:], ps[0]), axis=0, keepdims=True)
    hq = SA_HEADS * T
    carry = lax.fori_loop(0, n_pairs, lambda i, c: att_tiles([2 * i, 2 * i + 1], c),
                          (jnp.full((1, hq), -jnp.inf, F32), jnp.zeros((1, hq), F32)))
    _, den = lax.cond(odd, lambda c: att_tiles([nkt - 1], c), lambda c: c, carry)
    res = acc_ref[...] / den
    for hh in range(SA_HEADS):
        o_ref[:, hh * SA_DHEAD:(hh + 1) * SA_DHEAD] = res[:, hh * T:(hh + 1) * T].T[:, SA_DHEAD:]


def _sa_call(sa):
    b, s, _ = sa.shape
    topk = min(IDX_TOPK_MAX, s // 4)
    T = Q_BLOCK
    assert s % SA_KEY_TILE == 0
    qw = SA_HEADS * LANES
    seq = lambda blk: pl.BlockSpec((None, s, LANES), lambda i, j, blk=blk: (i, 0, blk))
    return pl.pallas_call(
        functools.partial(_sa_kernel, topk=topk),
        grid=(b, s // T),
        in_specs=[pl.BlockSpec((None, T, qw), lambda i, j: (i, j, 0)),
                  pl.BlockSpec((None, T, qw), lambda i, j: (i, j, 1)),
                  pl.BlockSpec((None, T, LANES), lambda i, j: (i, j, 2 * SA_HEADS + 1)),
                  seq(2 * SA_HEADS), seq(2 * SA_HEADS + 1)],
        out_specs=pl.BlockSpec((None, T, SA_WIDTH), lambda i, j: (i, j, 0)),
        out_shape=jax.ShapeDtypeStruct((b, s, SA_WIDTH), F32),
        scratch_shapes=[pltpu.VMEM((s, LANES), BF16), pltpu.VMEM((LANES, s), BF16),
                        pltpu.VMEM((s, LANES), BF16), pltpu.VMEM((s, T), jnp.int32),
                        pltpu.VMEM((s, T), F32), pltpu.VMEM((LANES, SA_HEADS * T), F32),
                        pltpu.VMEM((SA_KEY_TILE, SA_KEY_TILE), BF16),
                        pltpu.VMEM((32, s // 32, T), jnp.int32)],
        compiler_params=pltpu.CompilerParams(dimension_semantics=("arbitrary", "arbitrary"),
                                             vmem_limit_bytes=VMEM_LIMIT),
        name="dsa",
    )(sa, sa, sa, sa, sa)


def _mix_ffn_kernel(x_ref, oa_ref, ob_ref, oc_ref, wo_ref, gpost_ref, gpre_ref,
                    wup_ref, cw_ref, wdn_ref, gout_ref, o_ref, carry_ref, act_ref, *,
                    tiles_per_seq):
    tm = x_ref.shape[0]
    d_ff = wdn_ref.shape[0]
    assert cw_ref.shape[0] == FFN_CONV == 3
    first = pl.program_id(0) % tiles_per_seq == 0
    mixed = jnp.concatenate([oa_ref[...], ob_ref[...], oc_ref[...]], axis=1).astype(BF16)
    mix = jnp.dot(mixed, wo_ref[...], preferred_element_type=F32)
    x1 = x_ref[...] + _rms(mix, gpost_ref[...])
    h = _rms(x1, gpre_ref[...]).astype(BF16)
    row = lax.broadcasted_iota(jnp.int32, (tm, FF_TILE), 0)

    def conv(c0):
        u = jnp.dot(h, wup_ref[:, c0:c0 + FF_TILE], preferred_element_type=F32)
        prev = jnp.where(first, 0.0, carry_ref[:, c0:c0 + FF_TILE])
        carry_ref[:, c0:c0 + FF_TILE] = u[tm - 8:, :]
        u1 = jnp.where(row == 0, prev[7:8, :], pltpu.roll(u, 1, 0))
        u2 = jnp.where(row == 0, prev[6:7, :],
                       jnp.where(row == 1, prev[7:8, :], pltpu.roll(u, 2, 0)))
        w = cw_ref[:, c0:c0 + FF_TILE]
        return u2 * w[0:1, :] + u1 * w[1:2, :] + u * w[2:3, :]

    for c in range(d_ff // FF_TILE):
        gate = conv(c * FF_TILE)
        val = conv(d_ff + c * FF_TILE)
        inner = gate + 0.044715 * (gate * gate * gate)
        act = 0.5 * gate * (1.0 + jnp.tanh(0.7978845608028654 * inner)) * val
        act_ref[:, c * FF_TILE:(c + 1) * FF_TILE] = act.astype(BF16)
    down = jnp.dot(act_ref[...], wdn_ref[...], preferred_element_type=F32)
    o_ref[...] = x1 + _rms(down, gout_ref[...])


def _mix_ffn_call(x2, oa, ob, oc, wo, g_post, g_pre, w_up, conv_w, w_down, g_out, seq,
                  tm=512):
    m, d = x2.shape
    rows = lambda a: pl.BlockSpec((tm, a.shape[1]), lambda i: (i, 0))
    full = lambda a: pl.BlockSpec(a.shape, lambda i: (0, 0), pipeline_mode=pl.Buffered(1))
    return pl.pallas_call(
        functools.partial(_mix_ffn_kernel, tiles_per_seq=seq // tm),
        grid=(m // tm,),
        in_specs=[rows(x2), rows(oa), rows(ob), rows(oc), full(wo),
                  full(g_post), full(g_pre), full(w_up), full(conv_w), full(w_down), full(g_out)],
        out_specs=pl.BlockSpec((tm, d), lambda i: (i, 0)),
        out_shape=jax.ShapeDtypeStruct((m, d), F32),
        scratch_shapes=[pltpu.VMEM((8, w_up.shape[1]), F32),
                        pltpu.VMEM((tm, w_down.shape[0]), BF16)],
        compiler_params=pltpu.CompilerParams(dimension_semantics=("arbitrary",),
                                             vmem_limit_bytes=VMEM_LIMIT),
        name="mix_ffn",
    )(x2, oa, ob, oc, wo, g_post, g_pre, w_up, conv_w, w_down, g_out)


def kernel(x, w_in, dn_conv, dn_a_log, dn_dt_bias, dn_norm, hg_lb, hg_norm, w_out,
           g_mix_pre, g_mix_post, g_ffn_pre, g_ffn_post, ffn_w_up, ffn_conv, ffn_w_down):
    b, s, d = x.shape
    depth = w_in.shape[0]
    x2 = x.reshape(b * s, d)
    for l in range(depth):
        dn, sa, hg = _proj_call(x2, g_mix_pre[l].reshape(1, d), w_in[l].astype(BF16))
        o_a = _dn_call(dn.reshape(b, s, -1), dn_conv[l], dn_a_log[l], dn_dt_bias[l], dn_norm[l])
        o_b = _sa_call(sa.reshape(b, s, -1))
        o_c = _hg_call(hg.reshape(b, s, -1), hg_lb, hg_norm[l], l)
        x2 = _mix_ffn_call(
            x2, o_a.reshape(b * s, -1), o_b.reshape(b * s, -1), o_c.reshape(b * s, -1),
            w_out[l].astype(BF16),
            g_mix_post[l].reshape(1, d), g_ffn_pre[l].reshape(1, d),
            ffn_w_up[l].astype(BF16), ffn_conv[l], ffn_w_down[l].astype(BF16),
            g_ffn_post[l].reshape(1, d), s)
    return x2.reshape(b, s, d)
```

```python
import functools

import jax
import jax.numpy as jnp
from jax import lax
from jax.experimental import pallas as pl
from jax.experimental.pallas import tpu as pltpu

F32 = jnp.float32
BF16 = jnp.bfloat16
HIGHEST = lax.Precision.HIGHEST

LANES = 128
CHUNK = 64
CHUNK_SHIFT = 6
RMS_EPS = 1e-6
DN_DHEAD = 128
DN_HEADS = 4
DN_WIDTH = DN_HEADS * DN_DHEAD
DN_CONV = 4
SA_DHEAD = 64
SA_HEADS = 4
SA_WIDTH = SA_HEADS * SA_DHEAD
IDX_HEADS = 4
IDX_DHEAD = 64
IDX_TOPK_MAX = 256
Q_BLOCK = 256
SA_KEY_TILE = 512
HG_DHEAD = 64
HG_HEADS = 4
HG_WIDTH = HG_HEADS * HG_DHEAD
HG_SUB = 16
FFN_CONV = 3
FF_TILE = 256
SEQ_TILE = 1024
DN_GROUP = 4
HG_GROUP = 4
INT_MIN = -2 ** 31

VMEM_LIMIT = 56 * 1024 * 1024

DN_DHEAD_SHIFT = 7
assert CHUNK == 1 << CHUNK_SHIFT and DN_DHEAD == 1 << DN_DHEAD_SHIFT


def _mm(a, b):
    return jnp.dot(a.astype(BF16), b.astype(BF16), preferred_element_type=F32)


def _mm_nt(a, b):
    return lax.dot_general(a.astype(BF16), b.astype(BF16), (((1,), (1,)), ((), ())),
                           preferred_element_type=F32)


def _mm_f32(a, b):
    return jnp.dot(a, b, precision=HIGHEST, preferred_element_type=F32)


def _sigmoid(x):
    return 1.0 / (1.0 + jnp.exp(-x))


def _silu(x):
    return x * _sigmoid(x)


def _softplus(x):
    return jnp.maximum(x, 0.0) + jnp.log1p(jnp.exp(-jnp.abs(x)))


def _rms(x, g):
    return x * lax.rsqrt(jnp.mean(x * x, axis=-1, keepdims=True) + RMS_EPS) * g


DN_COLS = 4 * DN_WIDTH + LANES
SA_COLS = 2 * SA_HEADS * LANES + 2 * LANES
HG_COLS = 4 * HG_WIDTH


def _proj_kernel(x_ref, g_ref, w_ref, od_ref, os_ref, oh_ref):
    h = _rms(x_ref[...], g_ref[...]).astype(BF16)
    y = jnp.dot(h, w_ref[...], preferred_element_type=F32)
    zeros = lambda n: jnp.zeros((y.shape[0], n), F32)
    pos = [0]

    def take(n):
        piece = y[:, pos[0]:pos[0] + n]
        pos[0] += n
        return piece

    def padded_heads(n_heads, width):
        return [p for _ in range(n_heads) for p in (take(width), zeros(LANES - width))]

    od_ref[...] = jnp.concatenate([take(4 * DN_WIDTH + 2 * DN_HEADS), zeros(LANES - 2 * DN_HEADS)],
                                  axis=1)
    b_q = padded_heads(SA_HEADS, SA_DHEAD)
    b_kv = [take(2 * SA_DHEAD)]
    b_qi = padded_heads(IDX_HEADS, IDX_DHEAD)
    b_kiw = [take(IDX_DHEAD + IDX_HEADS), zeros(LANES - IDX_DHEAD - IDX_HEADS)]
    os_ref[...] = jnp.concatenate(b_q + b_qi + b_kv + b_kiw, axis=1)
    oh_ref[...] = take(HG_COLS)


def _proj_call(x2, g, w, tm=512):
    m, d = x2.shape
    full = lambda a: pl.BlockSpec(a.shape, lambda i: (0, 0), pipeline_mode=pl.Buffered(1))
    rows = lambda n: pl.BlockSpec((tm, n), lambda i: (i, 0))
    widths = (DN_COLS, SA_COLS, HG_COLS)
    return pl.pallas_call(
        _proj_kernel,
        grid=(m // tm,),
        in_specs=[rows(d), full(g), full(w)],
        out_specs=[rows(n) for n in widths],
        out_shape=[jax.ShapeDtypeStruct((m, n), F32) for n in widths],
        compiler_params=pltpu.CompilerParams(dimension_semantics=("arbitrary",),
                                             vmem_limit_bytes=VMEM_LIMIT),
        name="proj",
    )(x2, g, w)


def _dn_kernel(q_ref, k_ref, v_ref, z_ref, gt_ref, cw_ref, par_ref, ng_ref, o_ref,
               state_ref, halo_ref, u0_ref, lhs_ref, qkd_ref, kdt_ref, gend_ref):
    C = CHUNK
    D = DN_DHEAD
    H = DN_HEADS
    R = H * C
    n_chunks = q_ref.shape[0] // C
    lane = lax.broadcasted_iota(jnp.int32, (C, LANES), 1)
    row8 = lax.broadcasted_iota(jnp.int32, (8, DN_WIDTH), 0)
    r64 = lax.broadcasted_iota(jnp.int32, (C, C), 0)
    c64 = lax.broadcasted_iota(jnp.int32, (C, C), 1)
    tril = (c64 <= r64).astype(F32)
    rr = lax.broadcasted_iota(jnp.int32, (R, R), 0)
    cc = lax.broadcasted_iota(jnp.int32, (R, R), 1)
    same = (rr >> CHUNK_SHIFT) == (cc >> CHUNK_SHIFT)
    causal = same & (cc <= rr)
    strict = same & (cc < rr)
    eyef = (cc == rr).astype(F32)
    own_state = (lax.broadcasted_iota(jnp.int32, (R, H * D), 0) >> CHUNK_SHIFT) == \
        (lax.broadcasted_iota(jnp.int32, (R, H * D), 1) >> DN_DHEAD_SHIFT)
    neg_a = -jnp.exp(par_ref[0:1, :])
    dt_bias = par_ref[1:2, :]
    ng = ng_ref[...]
    srcs = (q_ref, k_ref, v_ref)

    @pl.when(pl.program_id(1) == 0)
    def _():
        state_ref[...] = jnp.zeros_like(state_ref)
        halo_ref[...] = jnp.zeros_like(halo_ref)

    def conv_silu(i, r0, n):
        ref = srcs[i]
        w = cw_ref[:, i * DN_WIDTH:(i + 1) * DN_WIDTH]
        cur = ref[pl.ds(r0, C), :]
        rp = pl.multiple_of(jnp.maximum(r0 - 8, 0), 8)
        prev = jnp.where(n > 0, ref[pl.ds(rp, 8), :], halo_ref[i])
        acc = cur * w[DN_CONV - 1:DN_CONV, :]
        for d in range(1, DN_CONV):
            rolled = pltpu.roll(cur, d, 0)
            head = jnp.where(row8 < d, pltpu.roll(prev, d, 0), rolled[:8, :])
            sh = jnp.concatenate([head, rolled[8:, :]], axis=0)
            acc = acc + sh * w[DN_CONV - 1 - d:DN_CONV - d, :]
        return _silu(acc)

    def l2norm(x):
        return x * lax.rsqrt(jnp.sum(x * x, axis=-1, keepdims=True) + 1e-6)

    def stack(a):
        return jnp.concatenate([a[:, h * D:(h + 1) * D] for h in range(H)], axis=0)

    def column(a, first_lane):
        return jnp.concatenate(
            [jnp.sum(jnp.where(lane == first_lane + h, a, 0.0), axis=1, keepdims=True)
             for h in range(H)], axis=0)

    def setup(n):
        r0 = pl.multiple_of(n * C, C)
        q = l2norm(stack(conv_silu(0, r0, n))) * (D ** -0.5)
        k = l2norm(stack(conv_silu(1, r0, n)))
        v = stack(conv_silu(2, r0, n))
        gt = gt_ref[pl.ds(r0, C), :]
        beta = column(_sigmoid(gt), 0)
        g_cum = _mm_f32(tril, neg_a * _softplus(gt + dt_bias))
        gb = jnp.broadcast_to(column(g_cum, H), (R, LANES))
        gcol = jnp.concatenate([gb, gb], axis=1)
        decay = jnp.exp(jnp.where(causal, gcol - gcol.T, -jnp.inf))

        qkk = _mm_nt(jnp.concatenate([q, k], axis=0), k)
        x = jnp.where(strict, -(beta * qkk[R:, :] * decay), 0.0)
        return x, (q, k, v, beta, gb, qkk, decay)

    def finish(n, t, aux):
        q, k, v, beta, gb, qkk, decay = aux
        eg = jnp.exp(gb)
        uw = _mm(t, jnp.concatenate([beta * v, (beta * eg) * k], axis=1))
        g_end = jnp.concatenate(
            [jnp.broadcast_to(gb[h * C + C - 1:(h + 1) * C, :], (C, LANES)) for h in range(H)],
            axis=0)
        qeg = q * eg
        u0_ref[n] = uw[:, :D]
        for h in range(H):
            rows = slice(h * C, (h + 1) * C)
            lhs_ref[n, h] = jnp.concatenate([uw[rows, D:], qeg[rows, :]], axis=0).astype(BF16)
        qkd_ref[n] = (qkk[:R, :] * decay).astype(BF16)
        kdt_ref[n] = (k * jnp.exp(g_end - gb)).T.astype(BF16)
        gend_ref[n] = jnp.concatenate(
            [jnp.broadcast_to(jnp.exp(gb[h * C + C - 1:(h + 1) * C, :]), (8, LANES))
             for h in range(H)], axis=1)

    def prepare(i, carry):
        chunks = [i * DN_GROUP + j for j in range(DN_GROUP)]
        xs, auxs = zip(*[setup(n) for n in chunks])
        ts = [eyef + x for x in xs]
        ps = list(xs)
        for _ in range(5):
            ps = [_mm(p, p) for p in ps]
            ts = [t + _mm(t, p) for t, p in zip(ts, ps)]
        for n, t, aux in zip(chunks, ts, auxs):
            finish(n, t, aux)
        return carry

    lax.fori_loop(0, n_chunks // DN_GROUP, prepare, 0)

    def advance(n, carry):
        r0 = pl.multiple_of(n * C, C)
        state = state_ref[...]
        ws = [jnp.dot(lhs_ref[n, h], state[:, h * D:(h + 1) * D].astype(BF16),
                      preferred_element_type=F32) for h in range(H)]
        u = u0_ref[n] - jnp.concatenate([w[:C, :] for w in ws], axis=0)
        o = jnp.concatenate([w[C:, :] for w in ws], axis=0) \
            + jnp.dot(qkd_ref[n], u.astype(BF16), preferred_element_type=F32)
        u_wide = jnp.where(own_state, jnp.concatenate([u] * H, axis=1), 0.0).astype(BF16)
        state_ref[...] = state * gend_ref[n][0:1, :] \
            + jnp.dot(kdt_ref[n], u_wide, preferred_element_type=F32)
        o = _rms(o, ng) * _silu(stack(z_ref[pl.ds(r0, C), :]))
        o_ref[pl.ds(r0, C), :] = jnp.concatenate(
            [o[h * C:(h + 1) * C, :] for h in range(H)], axis=1)
        return carry

    lax.fori_loop(0, n_chunks, advance, 0, unroll=2)
    last = q_ref.shape[0] - 8
    for i in range(3):
        halo_ref[i] = srcs[i][last:, :]


def _dn_call(dn, conv_w, a_log, dt_bias, norm_g):
    b, s, _ = dn.shape
    st = min(SEQ_TILE, s)
    nc = st // CHUNK
    rows = DN_HEADS * CHUNK
    par = jnp.zeros((8, LANES), F32)
    par = par.at[0, DN_HEADS:2 * DN_HEADS].set(a_log).at[1, DN_HEADS:2 * DN_HEADS].set(dt_bias)
    col = lambda blk: pl.BlockSpec((None, st, DN_WIDTH), lambda i, j, blk=blk: (i, j, blk))
    full = lambda a: pl.BlockSpec(a.shape, lambda i, j: (0, 0))
    ng = norm_g.reshape(1, LANES)
    return pl.pallas_call(
        _dn_kernel,
        grid=(b, s // st),
        in_specs=[col(0), col(1), col(2), col(3),
                  pl.BlockSpec((None, st, LANES), lambda i, j: (i, j, 4 * DN_HEADS)),
                  full(conv_w), full(par), full(ng)],
        out_specs=pl.BlockSpec((None, st, DN_WIDTH), lambda i, j: (i, j, 0)),
        out_shape=jax.ShapeDtypeStruct((b, s, DN_WIDTH), F32),
        scratch_shapes=[pltpu.VMEM((DN_DHEAD, DN_WIDTH), F32),
                        pltpu.VMEM((3, 8, DN_WIDTH), F32),
                        pltpu.VMEM((nc, rows, DN_DHEAD), F32),
                        pltpu.VMEM((nc, DN_HEADS, 2 * CHUNK, DN_DHEAD), BF16),
                        pltpu.VMEM((nc, rows, rows), BF16),
                        pltpu.VMEM((nc, DN_DHEAD, rows), BF16),
                        pltpu.VMEM((nc, 8, DN_WIDTH), F32)],
        compiler_params=pltpu.CompilerParams(dimension_semantics=("arbitrary", "arbitrary"),
                                             vmem_limit_bytes=VMEM_LIMIT),
        name="deltanet",
    )(dn, dn, dn, dn, dn, conv_w, par, ng)


def _hg_kernel(q_ref, f_ref, i_ref, g_ref, lb_ref, ng_ref, o_ref, state_ref, intra_ref,
               qdec_ref, kdec_ref, vt_ref, gend_ref, *, layer):
    C = CHUNK
    n_chunks = q_ref.shape[0] // C
    npair = HG_WIDTH // LANES
    nsub = C // HG_SUB
    lane = lax.broadcasted_iota(jnp.int32, (C, LANES), 1)
    row = lax.broadcasted_iota(jnp.int32, (C, LANES), 0)
    r64 = lax.broadcasted_iota(jnp.int32, (C, C), 0)
    c64 = lax.broadcasted_iota(jnp.int32, (C, C), 1)
    tril = (c64 <= r64).astype(F32)
    rl = lax.broadcasted_iota(jnp.int32, (LANES, LANES), 0)
    cl = lax.broadcasted_iota(jnp.int32, (LANES, LANES), 1)
    same_head = (rl < HG_DHEAD) == (cl < HG_DHEAD)
    head_ones = same_head.astype(BF16)
    head_mean = same_head.astype(F32) * (1.0 / HG_DHEAD)
    sub_row = lax.broadcasted_iota(jnp.int32, (HG_SUB, LANES), 0)
    pairs = []
    size = HG_SUB
    while size < C:
        pairs += [(first, size) for first in range(size, C, 2 * size)]
        size *= 2
    head0 = (lax.broadcasted_iota(jnp.int32, (C, LANES * len(pairs)), 1) & (LANES - 1)) < HG_DHEAD

    @pl.when(pl.program_id(1) == 0)
    def _():
        state_ref[...] = jnp.zeros_like(state_ref)

    lbr = lb_ref[...]
    e = jnp.exp(lbr - jnp.max(lbr, axis=0, keepdims=True))
    sm = e / jnp.sum(e, axis=0, keepdims=True)
    lb_all = jnp.zeros((1, HG_WIDTH), F32)
    for i in range(1, layer + 1):
        lb_all = lb_all + sm[i:i + 1, :]

    def load(unit):
        n, pr = unit
        r0 = pl.multiple_of(n * C, C)
        sl = slice(pr * LANES, (pr + 1) * LANES)
        lb = lb_all[:, sl]
        fg = lb + (1.0 - lb) * _sigmoid(f_ref[pl.ds(r0, C), sl])
        qc = _silu(q_ref[pl.ds(r0, C), sl])
        vc = i_ref[pl.ds(r0, C), sl]
        bc = _mm_f32(tril, jnp.log(fg))
        return qc, 1.0 - fg, vc, bc

    def diag_products(vals):
        qc, kc, vc, bc = vals
        parts = []
        for j in range(nsub):
            lo = j * HG_SUB
            bblk = bc[lo:lo + HG_SUB, :]
            qblk = qc[lo:lo + HG_SUB, :]
            for s in range(HG_SUB):
                dec = jnp.exp(jnp.where(sub_row >= s, bblk - bc[lo + s:lo + s + 1, :], -jnp.inf))
                parts.append((qblk * kc[lo + s:lo + s + 1, :] * dec).astype(BF16))
        return jnp.dot(jnp.concatenate(parts, axis=0), head_ones, preferred_element_type=F32)

    def off_diag_scores(vals):
        qc, kc, vc, bc = vals
        qts, kts = [], []
        for first, size in pairs:
            bref = bc[first:first + 1, :]
            qmask = (row >= first) & (row < first + size)
            kmask = (row >= first - size) & (row < first)
            qts.append(jnp.where(qmask, qc * jnp.exp(jnp.where(qmask, bc - bref, 0.0)), 0.0))
            kts.append(jnp.where(kmask, kc * jnp.exp(jnp.where(kmask, bref - bc, 0.0)), 0.0))
        qt = jnp.concatenate(qts, axis=1)
        kt = jnp.concatenate(kts, axis=1).astype(BF16)
        return _mm_nt(jnp.where(head0, qt, 0.0), kt), _mm_nt(jnp.where(head0, 0.0, qt), kt)

    def store(unit, vals, rs, a01):
        n, pr = unit
        qc, kc, vc, bc = vals
        diag = []
        for j in range(nsub):
            lo = j * HG_SUB
            acc = jnp.zeros((HG_SUB, LANES), F32)
            for s in range(HG_SUB):
                base = (j * HG_SUB + s) * HG_SUB
                acc = acc + rs[base:base + HG_SUB, :] * vc[lo + s:lo + s + 1, :]
            diag.append(acc)
        b_end = bc[C - 1:C, :]
        intra_ref[n, pr] = jnp.concatenate(diag, axis=0) \
            + jnp.where(lane < HG_DHEAD, _mm(a01[0], vc), _mm(a01[1], vc))
        qdec_ref[n, pr] = (qc * jnp.exp(bc)).astype(BF16)
        kdec_ref[n, pr] = (kc * jnp.exp(b_end - bc)).astype(BF16)
        vt_ref[n, pr] = vc.T.astype(BF16)
        gend_ref[n, pr] = jnp.broadcast_to(jnp.exp(b_end), (8, LANES))

    def prepare(i, carry):
        units = [(i * HG_GROUP + c, pr) for c in range(HG_GROUP) for pr in range(npair)]
        vals = [load(u) for u in units]
        rss = [diag_products(v) for v in vals]
        a01s = [off_diag_scores(v) for v in vals]
        for u, v, rs, a01 in zip(units, vals, rss, a01s):
            store(u, v, rs, a01)
        return carry

    lax.fori_loop(0, n_chunks // HG_GROUP, prepare, 0)

    def advance(i, carry):
        chunks = [i * HG_GROUP + c for c in range(HG_GROUP)]
        states = []
        for pr in range(npair):
            seq = [state_ref[pr]]
            for n in chunks:
                seq.append(jnp.where(
                    same_head,
                    seq[-1] * gend_ref[n, pr][0:1, :]
                    + jnp.dot(vt_ref[n, pr], kdec_ref[n, pr], preferred_element_type=F32), 0.0))
            state_ref[pr] = seq[-1]
            states.append(seq)
        os = [[intra_ref[n, pr] + lax.dot_general(
            qdec_ref[n, pr], states[pr][c].astype(BF16), (((1,), (1,)), ((), ())),
            preferred_element_type=F32) for pr in range(npair)] for c, n in enumerate(chunks)]
        mss = [[_mm_f32(o * o, head_mean) for o in row_os] for row_os in os]
        for c, n in enumerate(chunks):
            r0 = pl.multiple_of(n * C, C)
            o_ref[pl.ds(r0, C), :] = jnp.concatenate(
                [os[c][pr] * lax.rsqrt(mss[c][pr] + RMS_EPS) * ng_ref[...]
                 * _silu(g_ref[pl.ds(r0, C), pr * LANES:(pr + 1) * LANES])
                 for pr in range(npair)], axis=1)
        return carry

    lax.fori_loop(0, n_chunks // HG_GROUP, advance, 0)


def _hg_call(hg, hg_lb, norm_g, layer):
    b, s, _ = hg.shape
    st = min(SEQ_TILE, s)
    nc = st // CHUNK
    npair = HG_WIDTH // LANES
    col = lambda blk: pl.BlockSpec((None, st, HG_WIDTH), lambda i, j, blk=blk: (i, j, blk))
    full = lambda a: pl.BlockSpec(a.shape, lambda i, j: (0, 0))
    ng2 = jnp.tile(norm_g, LANES // HG_DHEAD).reshape(1, LANES)
    return pl.pallas_call(
        functools.partial(_hg_kernel, layer=layer),
        grid=(b, s // st),
        in_specs=[col(0), col(1), col(2), col(3), full(hg_lb), full(ng2)],
        out_specs=pl.BlockSpec((None, st, HG_WIDTH), lambda i, j: (i, j, 0)),
        out_shape=jax.ShapeDtypeStruct((b, s, HG_WIDTH), F32),
        scratch_shapes=[pltpu.VMEM((npair, LANES, LANES), F32),
                        pltpu.VMEM((nc, npair, CHUNK, LANES), F32),
                        pltpu.VMEM((nc, npair, CHUNK, LANES), BF16),
                        pltpu.VMEM((nc, npair, CHUNK, LANES), BF16),
                        pltpu.VMEM((nc, npair, LANES, CHUNK), BF16),
                        pltpu.VMEM((nc, npair, 8, LANES), F32)],
        compiler_params=pltpu.CompilerParams(dimension_semantics=("arbitrary", "arbitrary"),
                                             vmem_limit_bytes=VMEM_LIMIT),
        name="hgrn2",
    )(hg, hg, hg, hg, hg_lb, ng2)


def _sa_kernel(q_ref, qi_ref, wq_ref, kv_ref, kiw_ref, o_ref, kvb_ref, kvt_ref, kib_ref,
               key_ref, bias_ref, acc_ref, tri_ref, plane_ref, *, topk):
    T = Q_BLOCK
    TK = SA_KEY_TILE
    s_len = kv_ref.shape[0]
    j = pl.program_id(1)
    nkt = (j * T + T - 1) // TK + 1
    row_k = lax.broadcasted_iota(jnp.int32, (TK, T), 0)
    qpos = j * T + lax.broadcasted_iota(jnp.int32, (1, T), 1)
    first_hidden = ((qpos >> CHUNK_SHIFT) + 1) << CHUNK_SHIFT

    @pl.when(j == 0)
    def _():
        def prep(t, c):
            r0 = pl.multiple_of(t * TK, TK)
            kv = kv_ref[pl.ds(r0, TK), :]
            kvb_ref[pl.ds(r0, TK), :] = kv.astype(BF16)
            kvt_ref[:, pl.ds(r0, TK)] = kv.T.astype(BF16)
            kib_ref[pl.ds(r0, TK), :] = kiw_ref[pl.ds(r0, TK), :].astype(BF16)
            return c
        lax.fori_loop(0, s_len // TK, prep, 0)
        plane_ref[...] = jnp.zeros_like(plane_ref)
        tri_ref[...] = jnp.where(lax.broadcasted_iota(jnp.int32, (TK, TK), 1)
                                 <= lax.broadcasted_iota(jnp.int32, (TK, TK), 0),
                                 1.0, 0.0).astype(BF16)

    w_t = wq_ref[...].T * (IDX_HEADS ** -0.5 * IDX_DHEAD ** -0.5)
    w_all = jnp.concatenate(
        [w_t[IDX_DHEAD + hh:IDX_DHEAD + hh + 1, :] for hh in range(IDX_HEADS)], axis=1)

    def stack_heads(ref):
        return jnp.concatenate(
            [ref[:, hh * LANES:(hh + 1) * LANES] for hh in range(ref.shape[1] // LANES)], axis=0)
    qi_all = stack_heads(qi_ref).astype(BF16)

    n_pairs = nkt // 2
    odd = nkt % 2 == 1

    def score_tiles(kts):
        r0s = [pl.multiple_of(kt * TK, TK) for kt in kts]
        ds = [_mm_nt(kib_ref[pl.ds(r0, TK), :], qi_all) for r0 in r0s]
        for kt, r0, d in zip(kts, r0s, ds):
            d = jnp.maximum(d, 0.0) * w_all
            sc = d[:, :T]
            for hh in range(1, IDX_HEADS):
                sc = sc + d[:, hh * T:(hh + 1) * T]
            sc = jnp.where(sc == 0.0, 0.0, sc)
            bits = lax.bitcast_convert_type(sc, jnp.int32)
            key = jnp.where(bits < 0, bits ^ 0x7FFFFFFF, bits)
            key = jnp.where(kt * TK + row_k >= first_hidden, INT_MIN, key)
            key_ref[pl.ds(r0, TK), :] = key
            words = (key ^ INT_MIN).reshape(32, TK // 32, T)
            a = [words[m] for m in range(32)]
            dist, mask = 16, 0x0000FFFF
            while dist:
                for k in range(32):
                    if not k & dist:
                        t = (a[k] ^ (a[k + dist] >> dist)) & mask
                        a[k] = a[k] ^ t
                        a[k + dist] = a[k + dist] ^ (t << dist)
                dist >>= 1
                mask ^= mask << dist
            g0 = pl.multiple_of(kt * (TK // 32), TK // 32)
            for p in range(32):
                plane_ref[p, pl.ds(g0, TK // 32), :] = a[p]

    def score_pair(i, c):
        score_tiles([2 * i, 2 * i + 1])
        return c
    lax.fori_loop(0, n_pairs, score_pair, 0)

    @pl.when(odd)
    def _():
        score_tiles([nkt - 1])

    plane_rows = s_len // 32
    tile_of_row = lax.broadcasted_iota(jnp.int32, (plane_rows, T), 0) // (TK // 32)
    alive0 = jnp.where(tile_of_row < nkt, -1, 0).astype(jnp.int32)

    def ones_in(x):
        return jnp.sum(lax.population_count(x), axis=0, keepdims=True)

    def bit_step(i, carry):
        alive, thr_u, n_gt = carry
        ones = alive & plane_ref[i]
        c1 = ones_in(ones)
        take = n_gt + c1 >= topk
        bit = jnp.left_shift(jnp.int32(1), 31 - i)
        return (jnp.where(take, ones, alive ^ ones), jnp.where(take, thr_u | bit, thr_u),
                jnp.where(take, n_gt, n_gt + c1))
    zero = jnp.zeros((1, T), jnp.int32)
    alive, thr_u, n_gt = lax.fori_loop(0, 32, bit_step, (alive0, zero, zero))
    thr = thr_u ^ INT_MIN
    n_ge = n_gt + ones_in(alive)

    has_ties = jnp.max(((n_ge > topk) & (thr > INT_MIN)).astype(jnp.int32)) > 0

    def bias_with_ties():
        need = jnp.where(thr > INT_MIN, topk - n_gt, 0).astype(F32)

        def bias_tile(kt, seen):
            r0 = pl.multiple_of(kt * TK, TK)
            k = key_ref[pl.ds(r0, TK), :]
            tie = k == thr
            rank = seen + jnp.dot(tri_ref[...], jnp.where(tie, 1.0, 0.0).astype(BF16),
                                  preferred_element_type=F32)
            take = jnp.where(tie, rank, jnp.inf) <= need
            bias_ref[pl.ds(r0, TK), :] = jnp.where(k > thr, 0.0, jnp.where(take, 0.0, -jnp.inf))
            return rank[TK - 1:TK, :]
        lax.fori_loop(0, nkt, bias_tile, jnp.zeros((1, T), F32))

    def bias_without_ties():
        thr_lo = jnp.maximum(thr, INT_MIN + 1)

        def bias_tile(kt, c):
            r0 = pl.multiple_of(kt * TK, TK)
            bias_ref[pl.ds(r0, TK), :] = jnp.where(key_ref[pl.ds(r0, TK), :] >= thr_lo,
                                                   0.0, -jnp.inf)
            return c
        lax.fori_loop(0, nkt, bias_tile, 0)
    lax.cond(has_ties, bias_with_ties, bias_without_ties)

    acc_ref[...] = jnp.zeros_like(acc_ref)
    q_all = (stack_heads(q_ref) * (SA_DHEAD ** -0.5)).astype(BF16)

    def att_tiles(kts, carry):
        m_old, den = carry
        r0s = [pl.multiple_of(kt * TK, TK) for kt in kts]
        lgs = [_mm_nt(kvb_ref[pl.ds(r0, TK), :], q_all) for r0 in r0s]
        lgs = [lg + jnp.concatenate([bias_ref[pl.ds(r0, TK), :]] * SA_HEADS, axis=1)
               for lg, r0 in zip(lgs, r0s)]
        m_new = jnp.maximum(m_old, jnp.max(functools.reduce(jnp.maximum, lgs), axis=0,
                                           keepdims=True))
        m_use = jnp.where(m_new == -jnp.inf, 0.0, m_new)
        alpha = jnp.exp(m_old - m_use)
        ps = [jnp.exp(lg - m_use) for lg in lgs]
        pvs = [jnp.dot(kvt_ref[:, pl.ds(r0, TK)], p.astype(BF16), preferred_element_type=F32)
               for p, r0 in zip(ps, r0s)]
        acc_ref[...] = acc_ref[...] * alpha + sum(pvs[1:], pvs[0])
        return m_new, den * alpha + jnp.sum(sum(ps[1:], ps[0]), axis=0, keepdims=True)
    hq = SA_HEADS * T
    carry = lax.fori_loop(0, n_pairs, lambda i, c: att_tiles([2 * i, 2 * i + 1], c),
                          (jnp.full((1, hq), -jnp.inf, F32), jnp.zeros((1, hq), F32)))
    _, den = lax.cond(odd, lambda c: att_tiles([nkt - 1], c), lambda c: c, carry)
    res = acc_ref[...] / den
    for hh in range(SA_HEADS):
        o_ref[:, hh * SA_DHEAD:(hh + 1) * SA_DHEAD] = res[:, hh * T:(hh + 1) * T].T[:, SA_DHEAD:]


def _sa_call(sa):
    b, s, _ = sa.shape
    topk = min(IDX_TOPK_MAX, s // 4)
    T = Q_BLOCK
    assert s % SA_KEY_TILE == 0
    qw = SA_HEADS * LANES
    seq = lambda blk: pl.BlockSpec((None, s, LANES), lambda i, j, blk=blk: (i, 0, blk))
    return pl.pallas_call(
        functools.partial(_sa_kernel, topk=topk),
        grid=(b, s // T),
        in_specs=[pl.BlockSpec((None, T, qw), lambda i, j: (i, j, 0)),
                  pl.BlockSpec((None, T, qw), lambda i, j: (i, j, 1)),
                  pl.BlockSpec((None, T, LANES), lambda i, j: (i, j, 2 * SA_HEADS + 1)),
                  seq(2 * SA_HEADS), seq(2 * SA_HEADS + 1)],
        out_specs=pl.BlockSpec((None, T, SA_WIDTH), lambda i, j: (i, j, 0)),
        out_shape=jax.ShapeDtypeStruct((b, s, SA_WIDTH), F32),
        scratch_shapes=[pltpu.VMEM((s, LANES), BF16), pltpu.VMEM((LANES, s), BF16),
                        pltpu.VMEM((s, LANES), BF16), pltpu.VMEM((s, T), jnp.int32),
                        pltpu.VMEM((s, T), F32), pltpu.VMEM((LANES, SA_HEADS * T), F32),
                        pltpu.VMEM((SA_KEY_TILE, SA_KEY_TILE), BF16),
                        pltpu.VMEM((32, s // 32, T), jnp.int32)],
        compiler_params=pltpu.CompilerParams(dimension_semantics=("arbitrary", "arbitrary"),
                                             vmem_limit_bytes=VMEM_LIMIT),
        name="dsa",
    )(sa, sa, sa, sa, sa)


def _mix_ffn_kernel(x_ref, oa_ref, ob_ref, oc_ref, wo_ref, gpost_ref, gpre_ref,
                    wup_ref, cw_ref, wdn_ref, gout_ref, o_ref, carry_ref, act_ref, *,
                    tiles_per_seq):
    tm = x_ref.shape[0]
    d_ff = wdn_ref.shape[0]
    assert cw_ref.shape[0] == FFN_CONV == 3
    first = pl.program_id(0) % tiles_per_seq == 0
    mixed = jnp.concatenate([oa_ref[...], ob_ref[...], oc_ref[...]], axis=1).astype(BF16)
    mix = jnp.dot(mixed, wo_ref[...], preferred_element_type=F32)
    x1 = x_ref[...] + _rms(mix, gpost_ref[...])
    h = _rms(x1, gpre_ref[...]).astype(BF16)
    row = lax.broadcasted_iota(jnp.int32, (tm, FF_TILE), 0)

    def conv(c0):
        u = jnp.dot(h, wup_ref[:, c0:c0 + FF_TILE], preferred_element_type=F32)
        prev = jnp.where(first, 0.0, carry_ref[:, c0:c0 + FF_TILE])
        carry_ref[:, c0:c0 + FF_TILE] = u[tm - 8:, :]
        u1 = jnp.where(row == 0, prev[7:8, :], pltpu.roll(u, 1, 0))
        u2 = jnp.where(row == 0, prev[6:7, :],
                       jnp.where(row == 1, prev[7:8, :], pltpu.roll(u, 2, 0)))
        w = cw_ref[:, c0:c0 + FF_TILE]
        return u2 * w[0:1, :] + u1 * w[1:2, :] + u * w[2:3, :]

    for c in range(d_ff // FF_TILE):
        gate = conv(c * FF_TILE)
        val = conv(d_ff + c * FF_TILE)
        inner = gate + 0.044715 * (gate * gate * gate)
        act = 0.5 * gate * (1.0 + jnp.tanh(0.7978845608028654 * inner)) * val
        act_ref[:, c * FF_TILE:(c + 1) * FF_TILE] = act.astype(BF16)
    down = jnp.dot(act_ref[...], wdn_ref[...], preferred_element_type=F32)
    o_ref[...] = x1 + _rms(down, gout_ref[...])


def _mix_ffn_call(x2, oa, ob, oc, wo, g_post, g_pre, w_up, conv_w, w_down, g_out, seq,
                  tm=512):
    m, d = x2.shape
    rows = lambda a: pl.BlockSpec((tm, a.shape[1]), lambda i: (i, 0))
    full = lambda a: pl.BlockSpec(a.shape, lambda i: (0, 0), pipeline_mode=pl.Buffered(1))
    return pl.pallas_call(
        functools.partial(_mix_ffn_kernel, tiles_per_seq=seq // tm),
        grid=(m // tm,),
        in_specs=[rows(x2), rows(oa), rows(ob), rows(oc), full(wo),
                  full(g_post), full(g_pre), full(w_up), full(conv_w), full(w_down), full(g_out)],
        out_specs=pl.BlockSpec((tm, d), lambda i: (i, 0)),
        out_shape=jax.ShapeDtypeStruct((m, d), F32),
        scratch_shapes=[pltpu.VMEM((8, w_up.shape[1]), F32),
                        pltpu.VMEM((tm, w_down.shape[0]), BF16)],
        compiler_params=pltpu.CompilerParams(dimension_semantics=("arbitrary",),
                                             vmem_limit_bytes=VMEM_LIMIT),
        name="mix_ffn",
    )(x2, oa, ob, oc, wo, g_post, g_pre, w_up, conv_w, w_down, g_out)


def kernel(x, w_in, dn_conv, dn_a_log, dn_dt_bias, dn_norm, hg_lb, hg_norm, w_out,
           g_mix_pre, g_mix_post, g_ffn_pre, g_ffn_post, ffn_w_up, ffn_conv, ffn_w_down):
    b, s, d = x.shape
    depth = w_in.shape[0]
    x2 = x.reshape(b * s, d)
    for l in range(depth):
        dn, sa, hg = _proj_call(x2, g_mix_pre[l].reshape(1, d), w_in[l].astype(BF16))
        o_a = _dn_call(dn.reshape(b, s, -1), dn_conv[l], dn_a_log[l], dn_dt_bias[l], dn_norm[l])
        o_b = _sa_call(sa.reshape(b, s, -1))
        o_c = _hg_call(hg.reshape(b, s, -1), hg_lb, hg_norm[l], l)
        x2 = _mix_ffn_call(
            x2, o_a.reshape(b * s, -1), o_b.reshape(b * s, -1), o_c.reshape(b * s, -1),
            w_out[l].astype(BF16),
            g_mix_post[l].reshape(1, d), g_ffn_pre[l].reshape(1, d),
            ffn_w_up[l].astype(BF16), ffn_conv[l], ffn_w_down[l].astype(BF16),
            g_ffn_post[l].reshape(1, d), s)
    return x2.reshape(b, s, d)
```

```python
import functools

import jax
import jax.numpy as jnp
from jax import lax
from jax.experimental import pallas as pl
from jax.experimental.pallas import tpu as pltpu

F32 = jnp.float32
BF16 = jnp.bfloat16
HIGHEST = lax.Precision.HIGHEST

LANES = 128
CHUNK = 64
CHUNK_SHIFT = 6
RMS_EPS = 1e-6
DN_DHEAD = 128
DN_HEADS = 4
DN_WIDTH = DN_HEADS * DN_DHEAD
DN_CONV = 4
SA_DHEAD = 64
SA_HEADS = 4
SA_WIDTH = SA_HEADS * SA_DHEAD
IDX_HEADS = 4
IDX_DHEAD = 64
IDX_TOPK_MAX = 256
Q_BLOCK = 256
SA_KEY_TILE = 512
HG_DHEAD = 64
HG_HEADS = 4
HG_WIDTH = HG_HEADS * HG_DHEAD
HG_SUB = 16
FFN_CONV = 3
FF_TILE = 256
SEQ_TILE = 1024
DN_GROUP = 4
HG_GROUP = 8
INT_MIN = -2 ** 31

VMEM_LIMIT = 56 * 1024 * 1024

DN_DHEAD_SHIFT = 7
assert CHUNK == 1 << CHUNK_SHIFT and DN_DHEAD == 1 << DN_DHEAD_SHIFT


def _mm(a, b):
    return jnp.dot(a.astype(BF16), b.astype(BF16), preferred_element_type=F32)


def _mm_nt(a, b):
    return lax.dot_general(a.astype(BF16), b.astype(BF16), (((1,), (1,)), ((), ())),
                           preferred_element_type=F32)


def _mm_f32(a, b):
    return jnp.dot(a, b, precision=HIGHEST, preferred_element_type=F32)


def _sigmoid(x):
    return 1.0 / (1.0 + jnp.exp(-x))


def _silu(x):
    return x * _sigmoid(x)


def _softplus(x):
    return jnp.maximum(x, 0.0) + jnp.log1p(jnp.exp(-jnp.abs(x)))


def _rms(x, g):
    return x * lax.rsqrt(jnp.mean(x * x, axis=-1, keepdims=True) + RMS_EPS) * g


DN_COLS = 4 * DN_WIDTH + LANES
SA_COLS = 2 * SA_HEADS * LANES + 2 * LANES
HG_COLS = 4 * HG_WIDTH


def _proj_kernel(x_ref, g_ref, w_ref, od_ref, os_ref, oh_ref):
    h = _rms(x_ref[...], g_ref[...]).astype(BF16)
    y = jnp.dot(h, w_ref[...], preferred_element_type=F32)
    zeros = lambda n: jnp.zeros((y.shape[0], n), F32)
    pos = [0]

    def take(n):
        piece = y[:, pos[0]:pos[0] + n]
        pos[0] += n
        return piece

    def padded_heads(n_heads, width):
        return [p for _ in range(n_heads) for p in (take(width), zeros(LANES - width))]

    od_ref[...] = jnp.concatenate([take(4 * DN_WIDTH + 2 * DN_HEADS), zeros(LANES - 2 * DN_HEADS)],
                                  axis=1)
    b_q = padded_heads(SA_HEADS, SA_DHEAD)
    b_kv = [take(2 * SA_DHEAD)]
    b_qi = padded_heads(IDX_HEADS, IDX_DHEAD)
    b_kiw = [take(IDX_DHEAD + IDX_HEADS), zeros(LANES - IDX_DHEAD - IDX_HEADS)]
    os_ref[...] = jnp.concatenate(b_q + b_qi + b_kv + b_kiw, axis=1)
    oh_ref[...] = take(HG_COLS)


def _proj_call(x2, g, w, tm=512):
    m, d = x2.shape
    full = lambda a: pl.BlockSpec(a.shape, lambda i: (0, 0), pipeline_mode=pl.Buffered(1))
    rows = lambda n: pl.BlockSpec((tm, n), lambda i: (i, 0))
    widths = (DN_COLS, SA_COLS, HG_COLS)
    return pl.pallas_call(
        _proj_kernel,
        grid=(m // tm,),
        in_specs=[rows(d), full(g), full(w)],
        out_specs=[rows(n) for n in widths],
        out_shape=[jax.ShapeDtypeStruct((m, n), F32) for n in widths],
        compiler_params=pltpu.CompilerParams(dimension_semantics=("arbitrary",),
                                             vmem_limit_bytes=VMEM_LIMIT),
        name="proj",
    )(x2, g, w)


def _dn_kernel(q_ref, k_ref, v_ref, z_ref, gt_ref, cw_ref, par_ref, ng_ref, o_ref,
               state_ref, halo_ref, u0_ref, lhs_ref, qkd_ref, kdt_ref, gend_ref):
    C = CHUNK
    D = DN_DHEAD
    H = DN_HEADS
    R = H * C
    n_chunks = q_ref.shape[0] // C
    lane = lax.broadcasted_iota(jnp.int32, (C, LANES), 1)
    row8 = lax.broadcasted_iota(jnp.int32, (8, DN_WIDTH), 0)
    r64 = lax.broadcasted_iota(jnp.int32, (C, C), 0)
    c64 = lax.broadcasted_iota(jnp.int32, (C, C), 1)
    tril = (c64 <= r64).astype(F32)
    rr = lax.broadcasted_iota(jnp.int32, (R, R), 0)
    cc = lax.broadcasted_iota(jnp.int32, (R, R), 1)
    same = (rr >> CHUNK_SHIFT) == (cc >> CHUNK_SHIFT)
    causal = same & (cc <= rr)
    strict = same & (cc < rr)
    eyef = (cc == rr).astype(F32)
    own_state = (lax.broadcasted_iota(jnp.int32, (R, H * D), 0) >> CHUNK_SHIFT) == \
        (lax.broadcasted_iota(jnp.int32, (R, H * D), 1) >> DN_DHEAD_SHIFT)
    neg_a = -jnp.exp(par_ref[0:1, :])
    dt_bias = par_ref[1:2, :]
    ng = ng_ref[...]
    srcs = (q_ref, k_ref, v_ref)

    @pl.when(pl.program_id(1) == 0)
    def _():
        state_ref[...] = jnp.zeros_like(state_ref)
        halo_ref[...] = jnp.zeros_like(halo_ref)

    def conv_silu(i, r0, n):
        ref = srcs[i]
        w = cw_ref[:, i * DN_WIDTH:(i + 1) * DN_WIDTH]
        cur = ref[pl.ds(r0, C), :]
        rp = pl.multiple_of(jnp.maximum(r0 - 8, 0), 8)
        prev = jnp.where(n > 0, ref[pl.ds(rp, 8), :], halo_ref[i])
        acc = cur * w[DN_CONV - 1:DN_CONV, :]
        for d in range(1, DN_CONV):
            rolled = pltpu.roll(cur, d, 0)
            head = jnp.where(row8 < d, pltpu.roll(prev, d, 0), rolled[:8, :])
            sh = jnp.concatenate([head, rolled[8:, :]], axis=0)
            acc = acc + sh * w[DN_CONV - 1 - d:DN_CONV - d, :]
        return _silu(acc)

    def l2norm(x):
        return x * lax.rsqrt(jnp.sum(x * x, axis=-1, keepdims=True) + 1e-6)

    def stack(a):
        return jnp.concatenate([a[:, h * D:(h + 1) * D] for h in range(H)], axis=0)

    def column(a, first_lane):
        return jnp.concatenate(
            [jnp.sum(jnp.where(lane == first_lane + h, a, 0.0), axis=1, keepdims=True)
             for h in range(H)], axis=0)

    def setup(n):
        r0 = pl.multiple_of(n * C, C)
        q = l2norm(stack(conv_silu(0, r0, n))) * (D ** -0.5)
        k = l2norm(stack(conv_silu(1, r0, n)))
        v = stack(conv_silu(2, r0, n))
        gt = gt_ref[pl.ds(r0, C), :]
        beta = column(_sigmoid(gt), 0)
        g_cum = _mm_f32(tril, neg_a * _softplus(gt + dt_bias))
        gb = jnp.broadcast_to(column(g_cum, H), (R, LANES))
        gcol = jnp.concatenate([gb, gb], axis=1)
        decay = jnp.exp(jnp.where(causal, gcol - gcol.T, -jnp.inf))

        qkk = _mm_nt(jnp.concatenate([q, k], axis=0), k)
        x = jnp.where(strict, -(beta * qkk[R:, :] * decay), 0.0)
        return x, (q, k, v, beta, gb, qkk, decay)

    def finish(n, t, aux):
        q, k, v, beta, gb, qkk, decay = aux
        eg = jnp.exp(gb)
        uw = _mm(t, jnp.concatenate([beta * v, (beta * eg) * k], axis=1))
        g_end = jnp.concatenate(
            [jnp.broadcast_to(gb[h * C + C - 1:(h + 1) * C, :], (C, LANES)) for h in range(H)],
            axis=0)
        qeg = q * eg
        u0_ref[n] = uw[:, :D]
        for h in range(H):
            rows = slice(h * C, (h + 1) * C)
            lhs_ref[n, h] = jnp.concatenate([uw[rows, D:], qeg[rows, :]], axis=0).astype(BF16)
        qkd_ref[n] = (qkk[:R, :] * decay).astype(BF16)
        kdt_ref[n] = (k * jnp.exp(g_end - gb)).T.astype(BF16)
        gend_ref[n] = jnp.concatenate(
            [jnp.broadcast_to(jnp.exp(gb[h * C + C - 1:(h + 1) * C, :]), (8, LANES))
             for h in range(H)], axis=1)

    def prepare(i, carry):
        chunks = [i * DN_GROUP + j for j in range(DN_GROUP)]
        xs, auxs = zip(*[setup(n) for n in chunks])
        ts = [eyef + x for x in xs]
        ps = list(xs)
        for _ in range(5):
            ps = [_mm(p, p) for p in ps]
            ts = [t + _mm(t, p) for t, p in zip(ts, ps)]
        for n, t, aux in zip(chunks, ts, auxs):
            finish(n, t, aux)
        return carry

    lax.fori_loop(0, n_chunks // DN_GROUP, prepare, 0)

    def advance(n, carry):
        r0 = pl.multiple_of(n * C, C)
        state = state_ref[...]
        ws = [jnp.dot(lhs_ref[n, h], state[:, h * D:(h + 1) * D].astype(BF16),
                      preferred_element_type=F32) for h in range(H)]
        u = u0_ref[n] - jnp.concatenate([w[:C, :] for w in ws], axis=0)
        o = jnp.concatenate([w[C:, :] for w in ws], axis=0) \
            + jnp.dot(qkd_ref[n], u.astype(BF16), preferred_element_type=F32)
        u_wide = jnp.where(own_state, jnp.concatenate([u] * H, axis=1), 0.0).astype(BF16)
        state_ref[...] = state * gend_ref[n][0:1, :] \
            + jnp.dot(kdt_ref[n], u_wide, preferred_element_type=F32)
        o = _rms(o, ng) * _silu(stack(z_ref[pl.ds(r0, C), :]))
        o_ref[pl.ds(r0, C), :] = jnp.concatenate(
            [o[h * C:(h + 1) * C, :] for h in range(H)], axis=1)
        return carry

    lax.fori_loop(0, n_chunks, advance, 0, unroll=4)
    last = q_ref.shape[0] - 8
    for i in range(3):
        halo_ref[i] = srcs[i][last:, :]


def _dn_call(dn, conv_w, a_log, dt_bias, norm_g):
    b, s, _ = dn.shape
    st = min(SEQ_TILE, s)
    nc = st // CHUNK
    rows = DN_HEADS * CHUNK
    par = jnp.zeros((8, LANES), F32)
    par = par.at[0, DN_HEADS:2 * DN_HEADS].set(a_log).at[1, DN_HEADS:2 * DN_HEADS].set(dt_bias)
    col = lambda blk: pl.BlockSpec((None, st, DN_WIDTH), lambda i, j, blk=blk: (i, j, blk))
    full = lambda a: pl.BlockSpec(a.shape, lambda i, j: (0, 0))
    ng = norm_g.reshape(1, LANES)
    return pl.pallas_call(
        _dn_kernel,
        grid=(b, s // st),
        in_specs=[col(0), col(1), col(2), col(3),
                  pl.BlockSpec((None, st, LANES), lambda i, j: (i, j, 4 * DN_HEADS)),
                  full(conv_w), full(par), full(ng)],
        out_specs=pl.BlockSpec((None, st, DN_WIDTH), lambda i, j: (i, j, 0)),
        out_shape=jax.ShapeDtypeStruct((b, s, DN_WIDTH), F32),
        scratch_shapes=[pltpu.VMEM((DN_DHEAD, DN_WIDTH), F32),
                        pltpu.VMEM((3, 8, DN_WIDTH), F32),
                        pltpu.VMEM((nc, rows, DN_DHEAD), F32),
                        pltpu.VMEM((nc, DN_HEADS, 2 * CHUNK, DN_DHEAD), BF16),
                        pltpu.VMEM((nc, rows, rows), BF16),
                        pltpu.VMEM((nc, DN_DHEAD, rows), BF16),
                        pltpu.VMEM((nc, 8, DN_WIDTH), F32)],
        compiler_params=pltpu.CompilerParams(dimension_semantics=("arbitrary", "arbitrary"),
                                             vmem_limit_bytes=VMEM_LIMIT),
        name="deltanet",
    )(dn, dn, dn, dn, dn, conv_w, par, ng)


def _hg_kernel(q_ref, f_ref, i_ref, g_ref, lb_ref, ng_ref, o_ref, state_ref, intra_ref,
               qdec_ref, kdec_ref, vt_ref, gend_ref, *, layer):
    C = CHUNK
    n_chunks = q_ref.shape[0] // C
    npair = HG_WIDTH // LANES
    nsub = C // HG_SUB
    lane = lax.broadcasted_iota(jnp.int32, (C, LANES), 1)
    row = lax.broadcasted_iota(jnp.int32, (C, LANES), 0)
    r64 = lax.broadcasted_iota(jnp.int32, (C, C), 0)
    c64 = lax.broadcasted_iota(jnp.int32, (C, C), 1)
    tril = (c64 <= r64).astype(F32)
    rl = lax.broadcasted_iota(jnp.int32, (LANES, LANES), 0)
    cl = lax.broadcasted_iota(jnp.int32, (LANES, LANES), 1)
    same_head = (rl < HG_DHEAD) == (cl < HG_DHEAD)
    head_ones = same_head.astype(BF16)
    head_mean = same_head.astype(F32) * (1.0 / HG_DHEAD)
    sub_row = lax.broadcasted_iota(jnp.int32, (HG_SUB, LANES), 0)
    pairs = []
    size = HG_SUB
    while size < C:
        pairs += [(first, size) for first in range(size, C, 2 * size)]
        size *= 2
    head0 = (lax.broadcasted_iota(jnp.int32, (C, LANES * len(pairs)), 1) & (LANES - 1)) < HG_DHEAD

    @pl.when(pl.program_id(1) == 0)
    def _():
        state_ref[...] = jnp.zeros_like(state_ref)

    lbr = lb_ref[...]
    e = jnp.exp(lbr - jnp.max(lbr, axis=0, keepdims=True))
    sm = e / jnp.sum(e, axis=0, keepdims=True)
    lb_all = jnp.zeros((1, HG_WIDTH), F32)
    for i in range(1, layer + 1):
        lb_all = lb_all + sm[i:i + 1, :]

    def load(unit):
        n, pr = unit
        r0 = pl.multiple_of(n * C, C)
        sl = slice(pr * LANES, (pr + 1) * LANES)
        lb = lb_all[:, sl]
        fg = lb + (1.0 - lb) * _sigmoid(f_ref[pl.ds(r0, C), sl])
        qc = _silu(q_ref[pl.ds(r0, C), sl])
        vc = i_ref[pl.ds(r0, C), sl]
        bc = _mm_f32(tril, jnp.log(fg))
        return qc, 1.0 - fg, vc, bc

    def diag_products(vals):
        qc, kc, vc, bc = vals
        parts = []
        for j in range(nsub):
            lo = j * HG_SUB
            bblk = bc[lo:lo + HG_SUB, :]
            qblk = qc[lo:lo + HG_SUB, :]
            for s in range(HG_SUB):
                dec = jnp.exp(jnp.where(sub_row >= s, bblk - bc[lo + s:lo + s + 1, :], -jnp.inf))
                parts.append((qblk * kc[lo + s:lo + s + 1, :] * dec).astype(BF16))
        return jnp.dot(jnp.concatenate(parts, axis=0), head_ones, preferred_element_type=F32)

    def off_diag_scores(vals):
        qc, kc, vc, bc = vals
        qts, kts = [], []
        for first, size in pairs:
            bref = bc[first:first + 1, :]
            qmask = (row >= first) & (row < first + size)
            kmask = (row >= first - size) & (row < first)
            qts.append(jnp.where(qmask, qc * jnp.exp(jnp.where(qmask, bc - bref, 0.0)), 0.0))
            kts.append(jnp.where(kmask, kc * jnp.exp(jnp.where(kmask, bref - bc, 0.0)), 0.0))
        qt = jnp.concatenate(qts, axis=1)
        kt = jnp.concatenate(kts, axis=1).astype(BF16)
        return _mm_nt(jnp.where(head0, qt, 0.0), kt), _mm_nt(jnp.where(head0, 0.0, qt), kt)

    def store(unit, vals, rs, a01):
        n, pr = unit
        qc, kc, vc, bc = vals
        diag = []
        for j in range(nsub):
            lo = j * HG_SUB
            acc = jnp.zeros((HG_SUB, LANES), F32)
            for s in range(HG_SUB):
                base = (j * HG_SUB + s) * HG_SUB
                acc = acc + rs[base:base + HG_SUB, :] * vc[lo + s:lo + s + 1, :]
            diag.append(acc)
        b_end = bc[C - 1:C, :]
        intra_ref[n, pr] = jnp.concatenate(diag, axis=0) \
            + jnp.where(lane < HG_DHEAD, _mm(a01[0], vc), _mm(a01[1], vc))
        qdec_ref[n, pr] = (qc * jnp.exp(bc)).astype(BF16)
        kdec_ref[n, pr] = (kc * jnp.exp(b_end - bc)).astype(BF16)
        vt_ref[n, pr] = vc.T.astype(BF16)
        gend_ref[n, pr] = jnp.broadcast_to(jnp.exp(b_end), (8, LANES))

    def prepare(i, carry):
        units = [(i * HG_GROUP + c, pr) for c in range(HG_GROUP) for pr in range(npair)]
        vals = [load(u) for u in units]
        rss = [diag_products(v) for v in vals]
        a01s = [off_diag_scores(v) for v in vals]
        for u, v, rs, a01 in zip(units, vals, rss, a01s):
            store(u, v, rs, a01)
        return carry

    lax.fori_loop(0, n_chunks // HG_GROUP, prepare, 0)

    def advance(i, carry):
        chunks = [i * HG_GROUP + c for c in range(HG_GROUP)]
        states = []
        for pr in range(npair):
            seq = [state_ref[pr]]
            for n in chunks:
                seq.append(jnp.where(
                    same_head,
                    seq[-1] * gend_ref[n, pr][0:1, :]
                    + jnp.dot(vt_ref[n, pr], kdec_ref[n, pr], preferred_element_type=F32), 0.0))
            state_ref[pr] = seq[-1]
            states.append(seq)
        os = [[intra_ref[n, pr] + lax.dot_general(
            qdec_ref[n, pr], states[pr][c].astype(BF16), (((1,), (1,)), ((), ())),
            preferred_element_type=F32) for pr in range(npair)] for c, n in enumerate(chunks)]
        mss = [[_mm_f32(o * o, head_mean) for o in row_os] for row_os in os]
        for c, n in enumerate(chunks):
            r0 = pl.multiple_of(n * C, C)
            o_ref[pl.ds(r0, C), :] = jnp.concatenate(
                [os[c][pr] * lax.rsqrt(mss[c][pr] + RMS_EPS) * ng_ref[...]
                 * _silu(g_ref[pl.ds(r0, C), pr * LANES:(pr + 1) * LANES])
                 for pr in range(npair)], axis=1)
        return carry

    lax.fori_loop(0, n_chunks // HG_GROUP, advance, 0)


def _hg_call(hg, hg_lb, norm_g, layer):
    b, s, _ = hg.shape
    st = min(SEQ_TILE, s)
    nc = st // CHUNK
    npair = HG_WIDTH // LANES
    col = lambda blk: pl.BlockSpec((None, st, HG_WIDTH), lambda i, j, blk=blk: (i, j, blk))
    full = lambda a: pl.BlockSpec(a.shape, lambda i, j: (0, 0))
    ng2 = jnp.tile(norm_g, LANES // HG_DHEAD).reshape(1, LANES)
    return pl.pallas_call(
        functools.partial(_hg_kernel, layer=layer),
        grid=(b, s // st),
        in_specs=[col(0), col(1), col(2), col(3), full(hg_lb), full(ng2)],
        out_specs=pl.BlockSpec((None, st, HG_WIDTH), lambda i, j: (i, j, 0)),
        out_shape=jax.ShapeDtypeStruct((b, s, HG_WIDTH), F32),
        scratch_shapes=[pltpu.VMEM((npair, LANES, LANES), F32),
                        pltpu.VMEM((nc, npair, CHUNK, LANES), F32),
                        pltpu.VMEM((nc, npair, CHUNK, LANES), BF16),
                        pltpu.VMEM((nc, npair, CHUNK, LANES), BF16),
                        pltpu.VMEM((nc, npair, LANES, CHUNK), BF16),
                        pltpu.VMEM((nc, npair, 8, LANES), F32)],
        compiler_params=pltpu.CompilerParams(dimension_semantics=("arbitrary", "arbitrary"),
                                             vmem_limit_bytes=VMEM_LIMIT),
        name="hgrn2",
    )(hg, hg, hg, hg, hg_lb, ng2)


def _sa_kernel(q_ref, qi_ref, wq_ref, kv_ref, kiw_ref, o_ref, kvb_ref, kvt_ref, kib_ref,
               key_ref, bias_ref, acc_ref, tri_ref, plane_ref, *, topk):
    T = Q_BLOCK
    TK = SA_KEY_TILE
    s_len = kv_ref.shape[0]
    j = pl.program_id(1)
    nkt = (j * T + T - 1) // TK + 1
    row_k = lax.broadcasted_iota(jnp.int32, (TK, T), 0)
    qpos = j * T + lax.broadcasted_iota(jnp.int32, (1, T), 1)
    first_hidden = ((qpos >> CHUNK_SHIFT) + 1) << CHUNK_SHIFT

    @pl.when(j == 0)
    def _():
        def prep(t, c):
            r0 = pl.multiple_of(t * TK, TK)
            kv = kv_ref[pl.ds(r0, TK), :]
            kvb_ref[pl.ds(r0, TK), :] = kv.astype(BF16)
            kvt_ref[:, pl.ds(r0, TK)] = kv.T.astype(BF16)
            kib_ref[pl.ds(r0, TK), :] = kiw_ref[pl.ds(r0, TK), :].astype(BF16)
            return c
        lax.fori_loop(0, s_len // TK, prep, 0)
        plane_ref[...] = jnp.zeros_like(plane_ref)
        tri_ref[...] = jnp.where(lax.broadcasted_iota(jnp.int32, (TK, TK), 1)
                                 <= lax.broadcasted_iota(jnp.int32, (TK, TK), 0),
                                 1.0, 0.0).astype(BF16)

    w_t = wq_ref[...].T * (IDX_HEADS ** -0.5 * IDX_DHEAD ** -0.5)
    w_all = jnp.concatenate(
        [w_t[IDX_DHEAD + hh:IDX_DHEAD + hh + 1, :] for hh in range(IDX_HEADS)], axis=1)

    def stack_heads(ref):
        return jnp.concatenate(
            [ref[:, hh * LANES:(hh + 1) * LANES] for hh in range(ref.shape[1] // LANES)], axis=0)
    qi_all = stack_heads(qi_ref).astype(BF16)

    n_pairs = nkt // 2
    odd = nkt % 2 == 1

    def score_tiles(kts):
        r0s = [pl.multiple_of(kt * TK, TK) for kt in kts]
        ds = [_mm_nt(kib_ref[pl.ds(r0, TK), :], qi_all) for r0 in r0s]
        for kt, r0, d in zip(kts, r0s, ds):
            d = jnp.maximum(d, 0.0) * w_all
            sc = d[:, :T]
            for hh in range(1, IDX_HEADS):
                sc = sc + d[:, hh * T:(hh + 1) * T]
            sc = jnp.where(sc == 0.0, 0.0, sc)
            bits = lax.bitcast_convert_type(sc, jnp.int32)
            key = jnp.where(bits < 0, bits ^ 0x7FFFFFFF, bits)
            key = jnp.where(kt * TK + row_k >= first_hidden, INT_MIN, key)
            key_ref[pl.ds(r0, TK), :] = key
            words = (key ^ INT_MIN).reshape(32, TK // 32, T)
            a = [words[m] for m in range(32)]
            dist, mask = 16, 0x0000FFFF
            while dist:
                for k in range(32):
                    if not k & dist:
                        t = (a[k] ^ (a[k + dist] >> dist)) & mask
                        a[k] = a[k] ^ t
                        a[k + dist] = a[k + dist] ^ (t << dist)
                dist >>= 1
                mask ^= mask << dist
            g0 = pl.multiple_of(kt * (TK // 32), TK // 32)
            for p in range(32):
                plane_ref[p, pl.ds(g0, TK // 32), :] = a[p]

    def score_pair(i, c):
        score_tiles([2 * i, 2 * i + 1])
        return c
    lax.fori_loop(0, n_pairs, score_pair, 0)

    @pl.when(odd)
    def _():
        score_tiles([nkt - 1])

    plane_rows = s_len // 32
    tile_of_row = lax.broadcasted_iota(jnp.int32, (plane_rows, T), 0) // (TK // 32)
    alive0 = jnp.where(tile_of_row < nkt, -1, 0).astype(jnp.int32)

    def ones_in(x):
        return jnp.sum(lax.population_count(x), axis=0, keepdims=True)

    def bit_step(i, carry):
        alive, thr_u, n_gt = carry
        ones = alive & plane_ref[i]
        c1 = ones_in(ones)
        take = n_gt + c1 >= topk
        bit = jnp.left_shift(jnp.int32(1), 31 - i)
        return (jnp.where(take, ones, alive ^ ones), jnp.where(take, thr_u | bit, thr_u),
                jnp.where(take, n_gt, n_gt + c1))
    zero = jnp.zeros((1, T), jnp.int32)
    alive, thr_u, n_gt = lax.fori_loop(0, 32, bit_step, (alive0, zero, zero))
    thr = thr_u ^ INT_MIN
    n_ge = n_gt + ones_in(alive)

    has_ties = jnp.max(((n_ge > topk) & (thr > INT_MIN)).astype(jnp.int32)) > 0

    def bias_with_ties():
        need = jnp.where(thr > INT_MIN, topk - n_gt, 0).astype(F32)

        def bias_tile(kt, seen):
            r0 = pl.multiple_of(kt * TK, TK)
            k = key_ref[pl.ds(r0, TK), :]
            tie = k == thr
            rank = seen + jnp.dot(tri_ref[...], jnp.where(tie, 1.0, 0.0).astype(BF16),
                                  preferred_element_type=F32)
            take = jnp.where(tie, rank, jnp.inf) <= need
            bias_ref[pl.ds(r0, TK), :] = jnp.where(k > thr, 0.0, jnp.where(take, 0.0, -jnp.inf))
            return rank[TK - 1:TK, :]
        lax.fori_loop(0, nkt, bias_tile, jnp.zeros((1, T), F32))

    def bias_without_ties():
        thr_lo = jnp.maximum(thr, INT_MIN + 1)

        def bias_tile(kt, c):
            r0 = pl.multiple_of(kt * TK, TK)
            bias_ref[pl.ds(r0, TK), :] = jnp.where(key_ref[pl.ds(r0, TK), :] >= thr_lo,
                                                   0.0, -jnp.inf)
            return c
        lax.fori_loop(0, nkt, bias_tile, 0)
    lax.cond(has_ties, bias_with_ties, bias_without_ties)

    acc_ref[...] = jnp.zeros_like(acc_ref)
    q_all = (stack_heads(q_ref) * (SA_DHEAD ** -0.5)).astype(BF16)

    def att_tiles(kts, carry):
        m_old, den = carry
        r0s = [pl.multiple_of(kt * TK, TK) for kt in kts]
        lgs = [_mm_nt(kvb_ref[pl.ds(r0, TK), :], q_all) for r0 in r0s]
        lgs = [lg + jnp.concatenate([bias_ref[pl.ds(r0, TK), :]] * SA_HEADS, axis=1)
               for lg, r0 in zip(lgs, r0s)]
        m_new = jnp.maximum(m_old, jnp.max(functools.reduce(jnp.maximum, lgs), axis=0,
                                           keepdims=True))
        m_use = jnp.where(m_new == -jnp.inf, 0.0, m_new)
        alpha = jnp.exp(m_old - m_use)
        ps = [jnp.exp(lg - m_use) for lg in lgs]
        pvs = [jnp.dot(kvt_ref[:, pl.ds(r0, TK)], p.astype(BF16), preferred_element_type=F32)
               for p, r0 in zip(ps, r0s)]
        acc_ref[...] = acc_ref[...] * alpha + sum(pvs[1:], pvs[0])
        return m_new, den * alpha + jnp.sum(sum(ps[1:], ps[0]), axis=0, keepdims=True)
    hq = SA_HEADS * T
    carry = lax.fori_loop(0, n_pairs, lambda i, c: att_tiles([2 * i, 2 * i + 1], c),
                          (jnp.full((1, hq), -jnp.inf, F32), jnp.zeros((1, hq), F32)))
    _, den = lax.cond(odd, lambda c: att_tiles([nkt - 1], c), lambda c: c, carry)
    res = acc_ref[...] / den
    for hh in range(SA_HEADS):
        o_ref[:, hh * SA_DHEAD:(hh + 1) * SA_DHEAD] = res[:, hh * T:(hh + 1) * T].T[:, SA_DHEAD:]


def _sa_call(sa):
    b, s, _ = sa.shape
    topk = min(IDX_TOPK_MAX, s // 4)
    T = Q_BLOCK
    assert s % SA_KEY_TILE == 0
    qw = SA_HEADS * LANES
    seq = lambda blk: pl.BlockSpec((None, s, LANES), lambda i, j, blk=blk: (i, 0, blk))
    return pl.pallas_call(
        functools.partial(_sa_kernel, topk=topk),
        grid=(b, s // T),
        in_specs=[pl.BlockSpec((None, T, qw), lambda i, j: (i, j, 0)),
                  pl.BlockSpec((None, T, qw), lambda i, j: (i, j, 1)),
                  pl.BlockSpec((None, T, LANES), lambda i, j: (i, j, 2 * SA_HEADS + 1)),
                  seq(2 * SA_HEADS), seq(2 * SA_HEADS + 1)],
        out_specs=pl.BlockSpec((None, T, SA_WIDTH), lambda i, j: (i, j, 0)),
        out_shape=jax.ShapeDtypeStruct((b, s, SA_WIDTH), F32),
        scratch_shapes=[pltpu.VMEM((s, LANES), BF16), pltpu.VMEM((LANES, s), BF16),
                        pltpu.VMEM((s, LANES), BF16), pltpu.VMEM((s, T), jnp.int32),
                        pltpu.VMEM((s, T), F32), pltpu.VMEM((LANES, SA_HEADS * T), F32),
                        pltpu.VMEM((SA_KEY_TILE, SA_KEY_TILE), BF16),
                        pltpu.VMEM((32, s // 32, T), jnp.int32)],
        compiler_params=pltpu.CompilerParams(dimension_semantics=("arbitrary", "arbitrary"),
                                             vmem_limit_bytes=VMEM_LIMIT),
        name="dsa",
    )(sa, sa, sa, sa, sa)


def _mix_ffn_kernel(x_ref, oa_ref, ob_ref, oc_ref, wo_ref, gpost_ref, gpre_ref,
                    wup_ref, cw_ref, wdn_ref, gout_ref, o_ref, carry_ref, act_ref, *,
                    tiles_per_seq):
    tm = x_ref.shape[0]
    d_ff = wdn_ref.shape[0]
    assert cw_ref.shape[0] == FFN_CONV == 3
    first = pl.program_id(0) % tiles_per_seq == 0
    mixed = jnp.concatenate([oa_ref[...], ob_ref[...], oc_ref[...]], axis=1).astype(BF16)
    mix = jnp.dot(mixed, wo_ref[...], preferred_element_type=F32)
    x1 = x_ref[...] + _rms(mix, gpost_ref[...])
    h = _rms(x1, gpre_ref[...]).astype(BF16)
    row = lax.broadcasted_iota(jnp.int32, (tm, FF_TILE), 0)

    def conv(c0):
        u = jnp.dot(h, wup_ref[:, c0:c0 + FF_TILE], preferred_element_type=F32)
        prev = jnp.where(first, 0.0, carry_ref[:, c0:c0 + FF_TILE])
        carry_ref[:, c0:c0 + FF_TILE] = u[tm - 8:, :]
        u1 = jnp.where(row == 0, prev[7:8, :], pltpu.roll(u, 1, 0))
        u2 = jnp.where(row == 0, prev[6:7, :],
                       jnp.where(row == 1, prev[7:8, :], pltpu.roll(u, 2, 0)))
        w = cw_ref[:, c0:c0 + FF_TILE]
        return u2 * w[0:1, :] + u1 * w[1:2, :] + u * w[2:3, :]

    for c in range(d_ff // FF_TILE):
        gate = conv(c * FF_TILE)
        val = conv(d_ff + c * FF_TILE)
        inner = gate + 0.044715 * (gate * gate * gate)
        act = 0.5 * gate * (1.0 + jnp.tanh(0.7978845608028654 * inner)) * val
        act_ref[:, c * FF_TILE:(c + 1) * FF_TILE] = act.astype(BF16)
    down = jnp.dot(act_ref[...], wdn_ref[...], preferred_element_type=F32)
    o_ref[...] = x1 + _rms(down, gout_ref[...])


def _mix_ffn_call(x2, oa, ob, oc, wo, g_post, g_pre, w_up, conv_w, w_down, g_out, seq,
                  tm=512):
    m, d = x2.shape
    rows = lambda a: pl.BlockSpec((tm, a.shape[1]), lambda i: (i, 0))
    full = lambda a: pl.BlockSpec(a.shape, lambda i: (0, 0), pipeline_mode=pl.Buffered(1))
    return pl.pallas_call(
        functools.partial(_mix_ffn_kernel, tiles_per_seq=seq // tm),
        grid=(m // tm,),
        in_specs=[rows(x2), rows(oa), rows(ob), rows(oc), full(wo),
                  full(g_post), full(g_pre), full(w_up), full(conv_w), full(w_down), full(g_out)],
        out_specs=pl.BlockSpec((tm, d), lambda i: (i, 0)),
        out_shape=jax.ShapeDtypeStruct((m, d), F32),
        scratch_shapes=[pltpu.VMEM((8, w_up.shape[1]), F32),
                        pltpu.VMEM((tm, w_down.shape[0]), BF16)],
        compiler_params=pltpu.CompilerParams(dimension_semantics=("arbitrary",),
                                             vmem_limit_bytes=VMEM_LIMIT),
        name="mix_ffn",
    )(x2, oa, ob, oc, wo, g_post, g_pre, w_up, conv_w, w_down, g_out)


def kernel(x, w_in, dn_conv, dn_a_log, dn_dt_bias, dn_norm, hg_lb, hg_norm, w_out,
           g_mix_pre, g_mix_post, g_ffn_pre, g_ffn_post, ffn_w_up, ffn_conv, ffn_w_down):
    b, s, d = x.shape
    depth = w_in.shape[0]
    x2 = x.reshape(b * s, d)
    for l in range(depth):
        dn, sa, hg = _proj_call(x2, g_mix_pre[l].reshape(1, d), w_in[l].astype(BF16))
        o_a = _dn_call(dn.reshape(b, s, -1), dn_conv[l], dn_a_log[l], dn_dt_bias[l], dn_norm[l])
        o_b = _sa_call(sa.reshape(b, s, -1))
        o_c = _hg_call(hg.reshape(b, s, -1), hg_lb, hg_norm[l], l)
        x2 = _mix_ffn_call(
            x2, o_a.reshape(b * s, -1), o_b.reshape(b * s, -1), o_c.reshape(b * s, -1),
            w_out[l].astype(BF16),
            g_mix_post[l].reshape(1, d), g_ffn_pre[l].reshape(1, d),
            ffn_w_up[l].astype(BF16), ffn_conv[l], ffn_w_down[l].astype(BF16),
            g_ffn_post[l].reshape(1, d), s)
    return x2.reshape(b, s, d)
```
